```python
import math
import jax
import jax.numpy as jnp
from jax import lax
import numpy as np


D_MODEL = 1024
BATCH = 16
SEQ = 2048
DEPTH = 4

HEAD_DIM = 64
BLOCK_Q = 128
A_KV = 2
A_REP = 3
A_HEADS = A_KV * A_REP
CMP_LEN = 32
CMP_STRIDE = 16
CMP_HIDDEN = 256
SLC_LEN = 64
SLC_TOPN = 8
WIN_A = 512
B_PAIRS = ((128, 1), (512, 4), (2048, 16))
B_HEADS = 4
C_KV = 2
C_REP = 3
C_HEADS = C_KV * C_REP
WIN_C = 128
D_FF = 2816
CONV_W = 3
EPS = 1e-6

MIX_W = (A_HEADS + B_HEADS + C_HEADS) * HEAD_DIM
N_ALIBI = C_HEADS + len(B_PAIRS) * B_HEADS + A_HEADS
B_SLOPE_OFF = C_HEADS
A_SLOPE_OFF = C_HEADS + len(B_PAIRS) * B_HEADS
IN_SIZES = ((A_HEADS * HEAD_DIM,) + (A_KV * HEAD_DIM,) * 6 + (A_HEADS * 3,)
            + (len(B_PAIRS) * B_HEADS * HEAD_DIM,) * 3
            + (C_HEADS * HEAD_DIM, C_KV * HEAD_DIM, C_KV * HEAD_DIM))
D_IN = sum(IN_SIZES)
IN_SPLITS = tuple(sum(IN_SIZES[:i + 1]) for i in range(len(IN_SIZES) - 1))

kernel_name = 'hybrid_nsa_dilated_swa_sink_block'


def alibi_slopes():
    i = np.arange(1, N_ALIBI + 1, dtype=np.float32)
    return jnp.asarray(np.exp2(-8.0 * i / N_ALIBI).astype(np.float32))


def rmsnorm(x, g):
    xf = x.astype(jnp.float32)
    y = xf * lax.rsqrt(jnp.mean(xf * xf, axis=-1, keepdims=True) + EPS)
    return (y * g.astype(jnp.float32)).astype(x.dtype)


def banded_attention(q, k, v, past, slopes, dist_scale, sinks=None):
    n, L, G, R, hd = q.shape
    blk = math.gcd(L, BLOCK_Q)
    nb = L // blk
    npb = -(-past // blk)
    P = npb * blk
    pad = ((0, 0), (P, 0), (0, 0), (0, 0))
    kc = jnp.pad(k, pad).reshape(n, npb + nb, blk, G, hd)
    vc = jnp.pad(v, pad).reshape(n, npb + nb, blk, G, hd)
    kb = jnp.concatenate([kc[:, j:j + nb] for j in range(npb + 1)], axis=2)
    vb = jnp.concatenate([vc[:, j:j + nb] for j in range(npb + 1)], axis=2)
    K = (npb + 1) * blk
    qb = q.reshape(n, nb, blk, G, R, hd)
    s = jnp.einsum('nbqgrd,nbkgd->ngrbqk', qb, kb, preferred_element_type=jnp.float32) * (hd ** -0.5)
    diff = jnp.arange(blk)[:, None] - jnp.arange(K)[None, :] + P
    kpos = (jnp.arange(nb) * blk - P)[:, None, None] + jnp.arange(K)[None, None, :]
    mask = (diff >= 0) & (diff <= past) & (kpos >= 0)
    bias = -slopes.astype(jnp.float32)[:, :, None, None, None] * (diff * dist_scale).astype(jnp.float32)
    s = jnp.where(mask, s + bias, -jnp.inf)
    m = jnp.max(s, axis=-1)
    if sinks is not None:
        sk = sinks.astype(jnp.float32)[:, :, None, None]
        m = jnp.maximum(m, sk)
    p = jnp.exp(s - m[..., None])
    den = jnp.sum(p, axis=-1)
    if sinks is not None:
        den = den + jnp.exp(sk - m)
    o = jnp.einsum('ngrbqk,nbkgd->nbqgrd', p, vb.astype(jnp.float32))
    o = (o / den.transpose(0, 3, 4, 1, 2)[..., None]).reshape(n, L, G, R, hd)
    lse = (m + jnp.log(den)).transpose(0, 3, 4, 1, 2).reshape(n, L, G, R)
    return o.astype(q.dtype), lse


def block_overlap(nc, ns):
    cs = np.arange(nc)[:, None] * CMP_STRIDE
    ss = np.arange(ns)[None, :] * SLC_LEN
    ov = np.minimum(cs + CMP_LEN, ss + SLC_LEN) - np.maximum(cs, ss)
    return (np.clip(ov, 0, None) / CMP_STRIDE).astype(np.float32)


def compress(a, pos, w1, b1, w2):
    n, T, G, hd = a.shape
    n_sub = CMP_LEN // CMP_STRIDE
    nc = T // CMP_STRIDE - n_sub + 1
    ch = a.reshape(n, T // CMP_STRIDE, CMP_STRIDE, G, hd)
    blocks = jnp.concatenate([ch[:, j:j + nc] for j in range(n_sub)], axis=2)
    blocks = blocks + pos[None, None, :, None, :]
    flat = blocks.transpose(0, 1, 3, 2, 4).reshape(n, nc, G, CMP_LEN * hd)
    return jax.nn.gelu(flat @ w1 + b1) @ w2


def nsa_mixer(q, kc_raw, vc_raw, ks, vs, kw, vw, gate_logits, pos,
              k_w1, k_b1, k_w2, v_w1, v_b1, v_w2, slopes):
    n, T, G, R, hd = q.shape
    scale = hd ** -0.5
    t = jnp.arange(T)
    kc = compress(kc_raw, pos, k_w1, k_b1, k_w2)
    vc = compress(vc_raw, pos, v_w1, v_b1, v_w2)
    nc = kc.shape[1]
    c_start = jnp.arange(nc) * CMP_STRIDE
    c_valid = (c_start + CMP_LEN - 1)[None, :] <= t[:, None]
    c_dist = t[:, None] - (c_start + (CMP_LEN - 1) / 2)[None, :]
    s = jnp.einsum('ntgrd,ncgd->ngrtc', q, kc, preferred_element_type=jnp.float32) * scale
    s = jnp.where(c_valid, s - slopes[:, :, None, None] * c_dist, -jnp.inf)
    m = jnp.max(s, axis=-1, keepdims=True)
    p = jnp.exp(s - jnp.where(jnp.isfinite(m), m, 0.0))
    den = jnp.sum(p, axis=-1, keepdims=True)
    p_cmp = p / jnp.where(den > 0, den, 1.0)
    o_cmp = jnp.einsum('ngrtc,ncgd->ntgrd', p_cmp, vc.astype(jnp.float32))
    ns = T // SLC_LEN
    n_sel = min(SLC_TOPN, ns)
    imp = jnp.einsum('ngrtc,cs->ngts', p_cmp, jnp.asarray(block_overlap(nc, ns)))
    j = jnp.arange(ns)[None, :]
    cur = (t // SLC_LEN)[:, None]
    valid = j * SLC_LEN <= t[:, None]
    forced = (j == 0) | (j == cur) | (j == cur - 1)
    score = jnp.where(valid, jnp.where(forced, jnp.inf, imp), -jnp.inf)
    _, idx = lax.top_k(score, n_sel)
    k_blk = ks.reshape(n, ns, SLC_LEN, G, hd).transpose(0, 3, 1, 2, 4)
    v_blk = vs.reshape(n, ns, SLC_LEN, G, hd).transpose(0, 3, 1, 2, 4)
    nqb = T // BLOCK_Q
    q_blocks = q.reshape(n, nqb, BLOCK_Q, G, R, hd).transpose(1, 0, 2, 3, 4, 5)
    i_blocks = idx.reshape(n, G, nqb, BLOCK_Q, n_sel).transpose(2, 0, 1, 3, 4)
    bi = jnp.arange(n)[:, None, None, None]
    gi = jnp.arange(G)[None, :, None, None]
    kk = jnp.arange(SLC_LEN)

    def selected_block(args):
        qb, ib, b = args
        kg = k_blk[bi, gi, ib]
        vg = v_blk[bi, gi, ib]
        tq = b * BLOCK_Q + jnp.arange(BLOCK_Q)
        dist = tq[None, None, :, None, None] - (ib[..., None] * SLC_LEN + kk)
        s = jnp.einsum('nqgrd,ngqjkd->ngrqjk', qb, kg, preferred_element_type=jnp.float32) * scale
        s = jnp.where((dist >= 0)[:, :, None],
                      s - slopes[None, :, :, None, None, None] * dist[:, :, None], -jnp.inf)
        p = jax.nn.softmax(s.reshape(n, G, R, BLOCK_Q, n_sel * SLC_LEN), axis=-1)
        p = p.reshape(n, G, R, BLOCK_Q, n_sel, SLC_LEN)
        return jnp.einsum('ngrqjk,ngqjkd->nqgrd', p, vg.astype(jnp.float32))

    o_slc = lax.map(selected_block, (q_blocks, i_blocks, jnp.arange(nqb)))
    o_slc = o_slc.transpose(1, 0, 2, 3, 4, 5).reshape(n, T, G, R, hd)
    o_win, _ = banded_attention(q, kw, vw, WIN_A - 1, slopes, 1)
    g = jax.nn.sigmoid(gate_logits.astype(jnp.float32)).reshape(n, T, G, R, 3)
    o = g[..., 0:1] * o_cmp + g[..., 1:2] * o_slc + g[..., 2:3] * o_win.astype(jnp.float32)
    return o.reshape(n, T, G * R * hd)


def dilated_mixer(q, k, v, slopes):
    n, T, _, H, hd = q.shape
    outs, lses = [], []
    for p, (w, r) in enumerate(B_PAIRS):
        def fold(a):
            return a.reshape(n, T // r, r, H, hd).transpose(0, 2, 1, 3, 4).reshape(n * r, T // r, H, hd)
        o, lse = banded_attention(fold(q[:, :, p])[:, :, :, None], fold(k[:, :, p]), fold(v[:, :, p]),
                                  w // r, slopes[p * H:(p + 1) * H].reshape(H, 1), r)
        outs.append(o.reshape(n, r, T // r, H, hd).transpose(0, 2, 1, 3, 4).reshape(n, T, H, hd))
        lses.append(lse.reshape(n, r, T // r, H).transpose(0, 2, 1, 3).reshape(n, T, H))
    alpha = jax.nn.softmax(jnp.stack(lses, axis=0), axis=0)
    o = jnp.einsum('pnth,pnthd->nthd', alpha, jnp.stack(outs, axis=0).astype(jnp.float32))
    return o.reshape(n, T, H * hd)


def conv_glu_ffn(h, w_up, conv_w, conv_b, w_down):
    u = h @ w_up
    u = lax.conv_general_dilated(u, conv_w[:, None, :], window_strides=(1,),
                                 padding=[(CONV_W - 1, 0)],
                                 dimension_numbers=('NWC', 'WIO', 'NWC'),
                                 feature_group_count=u.shape[-1]) + conv_b
    a, b = jnp.split(u, 2, axis=-1)
    return (jax.nn.gelu(a, approximate=True) * b) @ w_down


def setup_inputs(seed: int = 0) -> dict:
    key = jax.random.key(seed)
    ks = jax.random.split(key, 20)
    f32 = jnp.float32
    L = DEPTH

    def nrm(k, shape, scale):
        return jax.random.normal(k, shape, f32) * scale

    return {
        'x': nrm(ks[0], (BATCH, SEQ, D_MODEL), 1.0),
        'g_mix_pre': 1.0 + nrm(ks[1], (L, D_MODEL), 0.1),
        'g_mix_post': 1.0 + nrm(ks[2], (L, D_MODEL), 0.1),
        'g_ffn_pre': 1.0 + nrm(ks[3], (L, D_MODEL), 0.1),
        'g_ffn_post': 1.0 + nrm(ks[4], (L, D_MODEL), 0.1),
        'w_in': nrm(ks[5], (L, D_MODEL, D_IN), D_MODEL ** -0.5),
        'cmp_pos': nrm(ks[6], (L, CMP_LEN, HEAD_DIM), 0.5),
        'cmp_k_w1': nrm(ks[7], (L, CMP_LEN * HEAD_DIM, CMP_HIDDEN), (CMP_LEN * HEAD_DIM) ** -0.5),
        'cmp_k_b1': nrm(ks[8], (L, CMP_HIDDEN), 0.02),
        'cmp_k_w2': nrm(ks[9], (L, CMP_HIDDEN, HEAD_DIM), CMP_HIDDEN ** -0.5),
        'cmp_v_w1': nrm(ks[10], (L, CMP_LEN * HEAD_DIM, CMP_HIDDEN), (CMP_LEN * HEAD_DIM) ** -0.5),
        'cmp_v_b1': nrm(ks[11], (L, CMP_HIDDEN), 0.02),
        'cmp_v_w2': nrm(ks[12], (L, CMP_HIDDEN, HEAD_DIM), CMP_HIDDEN ** -0.5),
        'sinks': nrm(ks[13], (L, C_HEADS), 0.5),
        'w_out': nrm(ks[14], (L, MIX_W, D_MODEL), MIX_W ** -0.5),
        'w_up': nrm(ks[15], (L, D_MODEL, 2 * D_FF), D_MODEL ** -0.5),
        'conv_w': nrm(ks[16], (L, CONV_W, 2 * D_FF), CONV_W ** -0.5),
        'conv_b': nrm(ks[17], (L, 2 * D_FF), 0.02),
        'w_down': nrm(ks[18], (L, D_FF, D_MODEL), D_FF ** -0.5),
    }


def reference(x, g_mix_pre, g_mix_post, g_ffn_pre, g_ffn_post, w_in, cmp_pos,
              cmp_k_w1, cmp_k_b1, cmp_k_w2, cmp_v_w1, cmp_v_b1, cmp_v_w2, sinks,
              w_out, w_up, conv_w, conv_b, w_down):
    n, T, _ = x.shape
    slopes = alibi_slopes()
    sl_c = slopes[:C_HEADS].reshape(C_KV, C_REP)
    sl_b = slopes[B_SLOPE_OFF:A_SLOPE_OFF]
    sl_a = slopes[A_SLOPE_OFF:A_SLOPE_OFF + A_HEADS].reshape(A_KV, A_REP)
    for l in range(DEPTH):
        h = rmsnorm(x, g_mix_pre[l])
        z = h @ w_in[l]
        (aq, akc, avc, aks, avs, akw, avw, agate,
         bq, bk, bv, cq, ck, cv) = jnp.split(z, list(IN_SPLITS), axis=-1)
        akv = lambda a: a.reshape(n, T, A_KV, HEAD_DIM)
        o_a = nsa_mixer(aq.reshape(n, T, A_KV, A_REP, HEAD_DIM), akv(akc), akv(avc), akv(aks), akv(avs),
                        akv(akw), akv(avw), agate, cmp_pos[l],
                        cmp_k_w1[l], cmp_k_b1[l], cmp_k_w2[l], cmp_v_w1[l], cmp_v_b1[l], cmp_v_w2[l], sl_a)
        bshape = (n, T, len(B_PAIRS), B_HEADS, HEAD_DIM)
        o_b = dilated_mixer(bq.reshape(bshape), bk.reshape(bshape), bv.reshape(bshape), sl_b)
        o_c, _ = banded_attention(cq.reshape(n, T, C_KV, C_REP, HEAD_DIM),
                                  ck.reshape(n, T, C_KV, HEAD_DIM), cv.reshape(n, T, C_KV, HEAD_DIM),
                                  WIN_C - 1, sl_c, 1, sinks=sinks[l].reshape(C_KV, C_REP))
        o_c = o_c.reshape(n, T, C_HEADS * HEAD_DIM)
        mix = jnp.concatenate([o_a.astype(x.dtype), o_b.astype(x.dtype), o_c.astype(x.dtype)], axis=-1) @ w_out[l]
        x = x + rmsnorm(mix, g_mix_post[l])
        f = conv_glu_ffn(rmsnorm(x, g_ffn_pre[l]), w_up[l], conv_w[l], conv_b[l], w_down[l])
        x = x + rmsnorm(f, g_ffn_post[l])
    return x
```

```python
import functools

import numpy as np
import jax
import jax.numpy as jnp
from jax import lax
from jax.experimental import pallas as pl
from jax.experimental.pallas import tpu as pltpu

D_MODEL = 1024
HEAD_DIM = 64
BLOCK_Q = 128
A_KV, A_REP = 2, 3
A_HEADS = A_KV * A_REP
CMP_LEN, CMP_STRIDE, CMP_HIDDEN = 32, 16, 256
SLC_LEN, SLC_TOPN = 64, 8
WIN_A = 512
B_PAIRS = ((128, 1), (512, 4), (2048, 16))
B_HEADS = 4
C_KV, C_REP = 2, 3
C_HEADS = C_KV * C_REP
WIN_C = 128
D_FF = 2816
CONV_W = 3
EPS = 1e-6

N_ALIBI = C_HEADS + len(B_PAIRS) * B_HEADS + A_HEADS
B_SLOPE_OFF = C_HEADS
A_SLOPE_OFF = C_HEADS + len(B_PAIRS) * B_HEADS
IN_SIZES = ((A_HEADS * HEAD_DIM,) + (A_KV * HEAD_DIM,) * 6 + (A_HEADS * 3,)
            + (len(B_PAIRS) * B_HEADS * HEAD_DIM,) * 3
            + (C_HEADS * HEAD_DIM, C_KV * HEAD_DIM, C_KV * HEAD_DIM))
IN_OFFS = tuple(int(v) for v in np.cumsum((0,) + IN_SIZES))

_SLOPES = tuple(float(v) for v in
                np.exp2(-8.0 * np.arange(1, N_ALIBI + 1, dtype=np.float32) / N_ALIBI).astype(np.float32))

Z_AQ, Z_CQ = 0, 384
Z_AKC, Z_AVC, Z_AKS, Z_AKW, Z_CK, Z_CV = 768, 896, 1024, 1152, 1280, 1408
Z_BQ, Z_BK, Z_BV = 1536, 2304, 3072
Z_W = 3840
Z_CHUNK = 768
GATE_ROWS = 32

F32 = jnp.float32
BF16 = jnp.bfloat16
NEG_INF = float("-inf")

TM_PROJ = 512
TM_FFN = 1024
TF_FFN = 256
FFN_HALO = 16
KC_SLC = 256
VMEM_LIMIT = 56 * 1024 * 1024


def _dot(a, b):
    return jnp.dot(a, b, preferred_element_type=F32)


def _dot_nt(a, b):
    return lax.dot_general(a, b, (((1,), (1,)), ((), ())), preferred_element_type=F32)


def _rmsnorm(x, g):
    return x * lax.rsqrt(jnp.mean(x * x, axis=-1, keepdims=True) + EPS) * g


def _gelu_tanh(x):
    return 0.5 * x * (1.0 + jnp.tanh(0.7978845608028654 * (x + 0.044715 * (x * x * x))))


def _params(*sem):
    return pltpu.CompilerParams(dimension_semantics=sem, vmem_limit_bytes=VMEM_LIMIT)


def _in_proj_kernel(x_ref, g_ref, w_ref, wtv_ref, wtg_ref, z_ref, vt_ref, gt_ref):
    h = _rmsnorm(x_ref[0], g_ref[...]).astype(BF16)
    for c in range(0, Z_W, Z_CHUNK):
        z_ref[0, :, c:c + Z_CHUNK] = _dot(h, w_ref[:, c:c + Z_CHUNK]).astype(BF16)
    vt_ref[0] = _dot_nt(wtv_ref[...], h).astype(BF16)
    gt_ref[0] = _dot_nt(wtg_ref[...], h)


def _in_proj(x, g, wz, wtv, wtg):
    n, T, D = x.shape
    tm = TM_PROJ
    return pl.pallas_call(
        _in_proj_kernel,
        grid=(n, T // tm),
        in_specs=[
            pl.BlockSpec((1, tm, D), lambda b, i: (b, i, 0)),
            pl.BlockSpec((1, D), lambda b, i: (0, 0)),
            pl.BlockSpec((D, Z_W), lambda b, i: (0, 0)),
            pl.BlockSpec((4 * HEAD_DIM, D), lambda b, i: (0, 0)),
            pl.BlockSpec((GATE_ROWS, D), lambda b, i: (0, 0)),
        ],
        out_specs=[
            pl.BlockSpec((1, tm, Z_W), lambda b, i: (b, i, 0)),
            pl.BlockSpec((1, 4 * HEAD_DIM, tm), lambda b, i: (b, 0, i)),
            pl.BlockSpec((1, GATE_ROWS, tm), lambda b, i: (b, 0, i)),
        ],
        out_shape=[
            jax.ShapeDtypeStruct((n, T, Z_W), BF16),
            jax.ShapeDtypeStruct((n, 4 * HEAD_DIM, T), BF16),
            jax.ShapeDtypeStruct((n, GATE_ROWS, T), F32),
        ],
        compiler_params=_params("parallel", "parallel"),
        name="in_proj",
    )(x, g, wz, wtv, wtg)


def _compress_kernel(xk_ref, xv_ref, pos_ref, w1k_ref, b1k_ref, w2k_ref,
                     w1v_ref, b1v_ref, w2vt_ref, kc_ref, vct_ref):
    half = CMP_STRIDE * HEAD_DIM

    def hidden(x, w1_ref, b1_ref):
        y1 = _dot(x, w1_ref[0:half, :])
        y2 = _dot(x, w1_ref[half:2 * half, :])
        bias = _dot(pos_ref[...], w1_ref[...])[0:1] + b1_ref[...]
        nrow = y2.shape[0]
        return _gelu_tanh(y1 + pltpu.roll(y2, nrow - 1, 0) + bias).astype(BF16)

    kc_ref[0, 0] = _dot(hidden(xk_ref[0, 0], w1k_ref, b1k_ref), w2k_ref[...]).astype(BF16)
    vct_ref[0, 0] = _dot_nt(w2vt_ref[...], hidden(xv_ref[0, 0], w1v_ref, b1v_ref)).astype(BF16)


def _compress(xkv, pos, w1k, b1k, w2k, w1v, b1v, w2vt):
    n, _, nch, kdim = xkv.shape
    full = lambda shape: pl.BlockSpec(shape, lambda b, g: (0,) * len(shape))
    return pl.pallas_call(
        _compress_kernel,
        grid=(n, A_KV),
        in_specs=[
            pl.BlockSpec((1, 1, nch, kdim), lambda b, g: (b, g, 0, 0)),
            pl.BlockSpec((1, 1, nch, kdim), lambda b, g: (b, A_KV + g, 0, 0)),
            full(pos.shape), full(w1k.shape), full(b1k.shape), full(w2k.shape),
            full(w1v.shape), full(b1v.shape), full(w2vt.shape),
        ],
        out_specs=[
            pl.BlockSpec((1, 1, nch, HEAD_DIM), lambda b, g: (b, g, 0, 0)),
            pl.BlockSpec((1, 1, HEAD_DIM, nch), lambda b, g: (b, g, 0, 0)),
        ],
        out_shape=[
            jax.ShapeDtypeStruct((n, A_KV, nch, HEAD_DIM), BF16),
            jax.ShapeDtypeStruct((n, A_KV, HEAD_DIM, nch), BF16),
        ],
        compiler_params=_params("parallel", "parallel"),
        name="compress",
    )(xkv, xkv, pos, w1k, b1k, w2k, w1v, b1v, w2vt)


def _nsa_kernel(q_ref, kc_ref, vct_ref, ks_ref, kw_ref, vt_ref, gt_ref, ovt_ref, o_ref, sel_ref):
    qb = pl.program_id(1)
    t0 = qb * BLOCK_Q
    nl = A_REP * BLOCK_Q
    ncmp = kc_ref.shape[2]
    nsel = ovt_ref.shape[0]
    lane = lax.broadcasted_iota(jnp.int32, (1, nl), 1)
    t_row = t0 + (lane & (BLOCK_Q - 1))
    t_row_f = t_row.astype(F32)
    sig = jax.nn.sigmoid(gt_ref[0])
    q = q_ref[0]

    for g in range(A_KV):
        heads = [g * A_REP + r for r in range(A_REP)]
        q3 = jnp.concatenate([q[:, h * HEAD_DIM:(h + 1) * HEAD_DIM] for h in heads], axis=0) * 0.125
        sl = [_SLOPES[A_SLOPE_OFF + h] for h in heads]
        slope = jnp.where(lane < BLOCK_Q, sl[0], jnp.where(lane < 2 * BLOCK_Q, sl[1], sl[2]))

        st = _dot_nt(kc_ref[0, g], q3)
        cst = lax.broadcasted_iota(jnp.int32, (ncmp, nl), 0) * CMP_STRIDE
        valid = (cst + (CMP_LEN - 1)) <= t_row
        dist = t_row_f - (cst.astype(F32) + (CMP_LEN - 1) / 2)
        st = jnp.where(valid, st - slope * dist, NEG_INF)
        m = jnp.max(st, axis=0, keepdims=True)
        m = jnp.where(m == NEG_INF, 0.0, m)
        p = jnp.exp(st - m)
        den = jnp.sum(p, axis=0, keepdims=True)
        pc = p / jnp.where(den > 0, den, 1.0)
        o_cmp = _dot(vct_ref[0, g], pc.astype(BF16))
        psum = pc[:, 0:BLOCK_Q] + pc[:, BLOCK_Q:2 * BLOCK_Q] + pc[:, 2 * BLOCK_Q:3 * BLOCK_Q]
        imp = _dot(ovt_ref[...], psum.astype(BF16))

        j = lax.broadcasted_iota(jnp.int32, (nsel, BLOCK_Q), 0)
        t128 = t0 + lax.broadcasted_iota(jnp.int32, (nsel, BLOCK_Q), 1)
        cur = t128 // SLC_LEN
        forced = (j == 0) | (j == cur) | (j == cur - 1)
        score = jnp.where(j * SLC_LEN <= t128, jnp.where(forced, float("inf"), imp), NEG_INF)
        ahead = jnp.zeros((nsel, BLOCK_Q), jnp.int32)
        for k in range(nsel):
            rk = score[k:k + 1, :]
            ahead = ahead + ((rk > score) | ((rk == score) & (j > k))).astype(jnp.int32)
        sel = (ahead < SLC_TOPN).astype(F32)
        sel_ref[...] = jnp.concatenate([sel] * A_REP, axis=1)

        blocks_per_chunk = KC_SLC // SLC_LEN

        def slc_body(c, carry):
            m_i, l_i, acc = carry
            k0 = pl.multiple_of(c * KC_SLC, KC_SLC)
            kch = ks_ref[0, pl.ds(k0, KC_SLC), g * HEAD_DIM:(g + 1) * HEAD_DIM]
            s = _dot_nt(kch, q3)
            kpos = k0 + lax.broadcasted_iota(jnp.int32, (KC_SLC, nl), 0)
            d = t_row - kpos
            keep = jnp.concatenate(
                [jnp.broadcast_to(sel_ref[pl.ds(c * blocks_per_chunk + b, 1), :], (SLC_LEN, nl))
                 for b in range(blocks_per_chunk)], axis=0)
            s = jnp.where((keep > 0.5) & (d >= 0), s - slope * d.astype(F32), NEG_INF)
            m_new = jnp.maximum(m_i, jnp.max(s, axis=0, keepdims=True))
            m_safe = jnp.where(m_new == NEG_INF, 0.0, m_new)
            alpha = jnp.exp(m_i - m_safe)
            pj = jnp.exp(s - m_safe)
            l_new = alpha * l_i + jnp.sum(pj, axis=0, keepdims=True)
            vch = vt_ref[0, g * HEAD_DIM:(g + 1) * HEAD_DIM, pl.ds(k0, KC_SLC)]
            acc = alpha * acc + _dot(vch, pj.astype(BF16))
            return m_new, l_new, acc

        n_chunks = (t0 + BLOCK_Q + KC_SLC - 1) // KC_SLC
        _, l_s, acc_s = lax.fori_loop(
            0, n_chunks, slc_body,
            (jnp.full((1, nl), NEG_INF, F32), jnp.zeros((1, nl), F32), jnp.zeros((HEAD_DIM, nl), F32)))
        o_slc = acc_s / l_s

        npb = -(-(WIN_A - 1) // BLOCK_Q)
        kwid = (npb + 1) * BLOCK_Q
        kst = pl.multiple_of(jnp.maximum(qb - npb, 0) * BLOCK_Q, BLOCK_Q)
        kwin = kw_ref[0, pl.ds(kst, kwid), g * HEAD_DIM:(g + 1) * HEAD_DIM]
        s = _dot_nt(kwin, q3)
        d = t_row - (kst + lax.broadcasted_iota(jnp.int32, (kwid, nl), 0))
        s = jnp.where((d >= 0) & (d <= WIN_A - 1), s - slope * d.astype(F32), NEG_INF)
        m = jnp.max(s, axis=0, keepdims=True)
        p = jnp.exp(s - m)
        l_w = jnp.sum(p, axis=0, keepdims=True)
        vwin = vt_ref[0, (A_KV + g) * HEAD_DIM:(A_KV + g + 1) * HEAD_DIM, pl.ds(kst, kwid)]
        o_win = _dot(vwin, p.astype(BF16)) / l_w

        for r, h in enumerate(heads):
            cols = slice(r * BLOCK_Q, (r + 1) * BLOCK_Q)
            o = (sig[3 * h:3 * h + 1] * o_cmp[:, cols] + sig[3 * h + 1:3 * h + 2] * o_slc[:, cols]
                 + sig[3 * h + 2:3 * h + 3] * o_win[:, cols])
            o_ref[0, h * HEAD_DIM:(h + 1) * HEAD_DIM, :] = o


def _nsa(z, kc, vct, vt, gt, ovt):
    n, T, _ = z.shape
    ncmp = kc.shape[2]
    nsel = ovt.shape[0]
    kv_w = A_KV * HEAD_DIM
    return pl.pallas_call(
        _nsa_kernel,
        grid=(n, T // BLOCK_Q),
        in_specs=[
            pl.BlockSpec((1, BLOCK_Q, A_HEADS * HEAD_DIM), lambda b, i: (b, i, Z_AQ // (A_HEADS * HEAD_DIM))),
            pl.BlockSpec((1, A_KV, ncmp, HEAD_DIM), lambda b, i: (b, 0, 0, 0)),
            pl.BlockSpec((1, A_KV, HEAD_DIM, ncmp), lambda b, i: (b, 0, 0, 0)),
            pl.BlockSpec((1, T, kv_w), lambda b, i: (b, 0, Z_AKS // kv_w)),
            pl.BlockSpec((1, T, kv_w), lambda b, i: (b, 0, Z_AKW // kv_w)),
            pl.BlockSpec((1, 2 * kv_w, T), lambda b, i: (b, 0, 0)),
            pl.BlockSpec((1, GATE_ROWS, BLOCK_Q), lambda b, i: (b, 0, i)),
            pl.BlockSpec((nsel, ncmp), lambda b, i: (0, 0)),
        ],
        out_specs=pl.BlockSpec((1, A_HEADS * HEAD_DIM, BLOCK_Q), lambda b, i: (b, 0, i)),
        out_shape=jax.ShapeDtypeStruct((n, A_HEADS * HEAD_DIM, T), F32),
        scratch_shapes=[pltpu.VMEM((nsel, A_REP * BLOCK_Q), F32)],
        compiler_params=_params("parallel", "parallel"),
        name="nsa",
    )(z, kc, vct, z, z, vt, gt, ovt)


def _banded_kernel(*refs, groups, rep, past, kwid, slopes, dist_scale, has_sinks, want_lse, n_axes):
    refs = list(refs)
    sink_ref = refs.pop(0) if has_sinks else None
    q_ref, k_ref, v_ref, o_ref = refs[:4]
    lse_ref = refs[4] if want_lse else None
    qb = pl.program_id(n_axes - 1)
    npb = -(-past // BLOCK_Q)
    kst = pl.multiple_of(jnp.maximum(qb - npb, 0) * BLOCK_Q, BLOCK_Q)
    diff = ((qb * BLOCK_Q - kst) + lax.broadcasted_iota(jnp.int32, (BLOCK_Q, kwid), 0)
            - lax.broadcasted_iota(jnp.int32, (BLOCK_Q, kwid), 1))
    mask = (diff >= 0) & (diff <= past)
    dfl = (diff * dist_scale).astype(F32)
    q = q_ref[0]
    for g in range(groups):
        heads = [g * rep + r for r in range(rep)]
        kwin = k_ref[0, pl.ds(kst, kwid), g * HEAD_DIM:(g + 1) * HEAD_DIM]
        vwin = v_ref[0, pl.ds(kst, kwid), g * HEAD_DIM:(g + 1) * HEAD_DIM]
        qg = jnp.concatenate([q[:, h * HEAD_DIM:(h + 1) * HEAD_DIM] for h in heads], axis=0) * 0.125
        s = _dot_nt(qg, kwin)
        ps, dens, ms = [], [], []
        for r, h in enumerate(heads):
            sr = jnp.where(mask, s[r * BLOCK_Q:(r + 1) * BLOCK_Q] - slopes[h] * dfl, NEG_INF)
            m = jnp.max(sr, axis=1, keepdims=True)
            if has_sinks:
                m = jnp.maximum(m, sink_ref[h])
            p = jnp.exp(sr - m)
            den = jnp.sum(p, axis=1, keepdims=True)
            if has_sinks:
                den = den + jnp.exp(sink_ref[h] - m)
            ps.append(p.astype(BF16))
            dens.append(den)
            ms.append(m)
        o = _dot(jnp.concatenate(ps, axis=0), vwin)
        for r, h in enumerate(heads):
            cols = slice(h * HEAD_DIM, (h + 1) * HEAD_DIM)
            o_ref[0, :, cols] = (o[r * BLOCK_Q:(r + 1) * BLOCK_Q] / dens[r]).astype(o_ref.dtype)
            if want_lse:
                lse_ref[0, :, cols] = jnp.broadcast_to(ms[r] + jnp.log(dens[r]), (BLOCK_Q, HEAD_DIM))


def _swa_sink(z, sinks):
    n, T, _ = z.shape
    qw, kw = C_HEADS * HEAD_DIM, C_KV * HEAD_DIM
    past = WIN_C - 1
    kwid = min((-(-past // BLOCK_Q) + 1) * BLOCK_Q, T)
    kern = functools.partial(
        _banded_kernel, groups=C_KV, rep=C_REP, past=past, kwid=kwid, slopes=_SLOPES[:C_HEADS],
        dist_scale=1, has_sinks=True, want_lse=False, n_axes=2)
    return pl.pallas_call(
        kern,
        grid=(n, T // BLOCK_Q),
        in_specs=[
            pl.BlockSpec(memory_space=pltpu.SMEM),
            pl.BlockSpec((1, BLOCK_Q, qw), lambda b, i: (b, i, Z_CQ // qw)),
            pl.BlockSpec((1, T, kw), lambda b, i: (b, 0, Z_CK // kw)),
            pl.BlockSpec((1, T, kw), lambda b, i: (b, 0, Z_CV // kw)),
        ],
        out_specs=pl.BlockSpec((1, BLOCK_Q, qw), lambda b, i: (b, i, 0)),
        out_shape=jax.ShapeDtypeStruct((n, T, qw), BF16),
        compiler_params=_params("parallel", "parallel"),
        name="swa_sink",
    )(sinks, z, z, z)


def _dilated_pair(z, pair):
    n, T, _ = z.shape
    w, r = B_PAIRS[pair]
    tf = T // r
    bw = B_HEADS * HEAD_DIM
    zf = z.reshape(n, tf, r * Z_W)
    per_row = Z_W // bw
    past = w // r
    kwid = min((-(-past // BLOCK_Q) + 1) * BLOCK_Q, tf)
    hs = B_SLOPE_OFF + pair * B_HEADS
    kern = functools.partial(
        _banded_kernel, groups=B_HEADS, rep=1, past=past, kwid=kwid, slopes=_SLOPES[hs:hs + B_HEADS],
        dist_scale=r, has_sinks=False, want_lse=True, n_axes=3)
    o, lse = pl.pallas_call(
        kern,
        grid=(n, r, tf // BLOCK_Q),
        in_specs=[
            pl.BlockSpec((1, BLOCK_Q, bw), lambda b, j, i: (b, i, j * per_row + Z_BQ // bw + pair)),
            pl.BlockSpec((1, tf, bw), lambda b, j, i: (b, 0, j * per_row + Z_BK // bw + pair)),
            pl.BlockSpec((1, tf, bw), lambda b, j, i: (b, 0, j * per_row + Z_BV // bw + pair)),
        ],
        out_specs=[
            pl.BlockSpec((1, BLOCK_Q, bw), lambda b, j, i: (b, i, j)),
            pl.BlockSpec((1, BLOCK_Q, bw), lambda b, j, i: (b, i, j)),
        ],
        out_shape=[
            jax.ShapeDtypeStruct((n, tf, r * bw), F32),
            jax.ShapeDtypeStruct((n, tf, r * bw), F32),
        ],
        compiler_params=_params("parallel", "parallel", "parallel"),
        name=f"dilated_{pair}",
    )(zf, zf, zf)
    return o.reshape(n, T, bw), lse.reshape(n, T, bw)


def _out_proj_kernel(x_ref, oat_ref, ob0_ref, ob1_ref, ob2_ref, l0_ref, l1_ref, l2_ref,
                     oc_ref, w_ref, g_ref, y_ref):
    l0, l1, l2 = l0_ref[0], l1_ref[0], l2_ref[0]
    mx = jnp.maximum(jnp.maximum(l0, l1), l2)
    e0, e1, e2 = jnp.exp(l0 - mx), jnp.exp(l1 - mx), jnp.exp(l2 - mx)
    ob = (e0 * ob0_ref[0] + e1 * ob1_ref[0] + e2 * ob2_ref[0]) / (e0 + e1 + e2)
    wa = A_HEADS * HEAD_DIM
    wb = wa + B_HEADS * HEAD_DIM
    oa = oat_ref[0].T.astype(BF16)
    mix = (_dot(oa, w_ref[0:wa, :]) + _dot(ob.astype(BF16), w_ref[wa:wb, :])
           + _dot(oc_ref[0], w_ref[wb:, :]))
    y_ref[0] = x_ref[0] + _rmsnorm(mix, g_ref[...])


def _out_proj(x, oat, obs, lses, oc, w, g):
    n, T, D = x.shape
    tm = TM_PROJ
    wa, wb, wc = A_HEADS * HEAD_DIM, B_HEADS * HEAD_DIM, C_HEADS * HEAD_DIM
    row = lambda width: pl.BlockSpec((1, tm, width), lambda b, i: (b, i, 0))
    return pl.pallas_call(
        _out_proj_kernel,
        grid=(n, T // tm),
        in_specs=[
            row(D),
            pl.BlockSpec((1, wa, tm), lambda b, i: (b, 0, i)),
            row(wb), row(wb), row(wb), row(wb), row(wb), row(wb),
            row(wc),
            pl.BlockSpec((D, D), lambda b, i: (0, 0)),
            pl.BlockSpec((1, D), lambda b, i: (0, 0)),
        ],
        out_specs=row(D),
        out_shape=jax.ShapeDtypeStruct((n, T, D), F32),
        compiler_params=_params("parallel", "parallel"),
        name="out_proj",
    )(x, oat, *obs, *lses, oc, w, g)


def _ffn_kernel(x_ref, xp_ref, gpre_ref, wua_ref, wub_ref, cwa_ref, cwb_ref, cba_ref, cbb_ref,
                wd_ref, gpost_ref, y_ref, h_ref):
    ti = pl.program_id(1)
    j = pl.program_id(2)

    @pl.when(j == 0)
    def _():
        hp = _rmsnorm(xp_ref[0], gpre_ref[...])
        h_ref[0:FFN_HALO, :] = jnp.where(ti > 0, hp, 0.0).astype(BF16)
        h_ref[FFN_HALO:, :] = _rmsnorm(x_ref[0], gpre_ref[...]).astype(BF16)
        y_ref[0] = jnp.zeros(y_ref.shape[1:], F32)

    h = h_ref[...]

    def conv_branch(w_ref, cw_ref, cb_ref):
        u = _dot(h, w_ref[...])
        cw = cw_ref[...]
        y = cw[2:3] * u + cw[1:2] * pltpu.roll(u, 1, 0) + cw[0:1] * pltpu.roll(u, 2, 0)
        return y[FFN_HALO:] + cb_ref[...]

    a = conv_branch(wua_ref, cwa_ref, cba_ref)
    b = conv_branch(wub_ref, cwb_ref, cbb_ref)
    y_ref[0] += _dot((_gelu_tanh(a) * b).astype(BF16), wd_ref[...])

    @pl.when(j == pl.num_programs(2) - 1)
    def _():
        y_ref[0] = x_ref[0] + _rmsnorm(y_ref[0], gpost_ref[...])


def _ffn(x, gpre, w_up, conv_w, conv_b, w_down, gpost):
    n, T, D = x.shape
    tm, tf = TM_FFN, TF_FFN
    nff = D_FF // tf
    halo_blocks = tm // FFN_HALO
    return pl.pallas_call(
        _ffn_kernel,
        grid=(n, T // tm, nff),
        in_specs=[
            pl.BlockSpec((1, tm, D), lambda b, i, j: (b, i, 0)),
            pl.BlockSpec((1, FFN_HALO, D), lambda b, i, j: (b, jnp.maximum(i * halo_blocks - 1, 0), 0)),
            pl.BlockSpec((1, D), lambda b, i, j: (0, 0)),
            pl.BlockSpec((D, tf), lambda b, i, j: (0, j)),
            pl.BlockSpec((D, tf), lambda b, i, j: (0, nff + j)),
            pl.BlockSpec((CONV_W, tf), lambda b, i, j: (0, j)),
            pl.BlockSpec((CONV_W, tf), lambda b, i, j: (0, nff + j)),
            pl.BlockSpec((1, tf), lambda b, i, j: (0, j)),
            pl.BlockSpec((1, tf), lambda b, i, j: (0, nff + j)),
            pl.BlockSpec((tf, D), lambda b, i, j: (j, 0)),
            pl.BlockSpec((1, D), lambda b, i, j: (0, 0)),
        ],
        out_specs=pl.BlockSpec((1, tm, D), lambda b, i, j: (b, i, 0)),
        out_shape=jax.ShapeDtypeStruct((n, T, D), F32),
        scratch_shapes=[pltpu.VMEM((FFN_HALO + tm, D), BF16)],
        compiler_params=_params("parallel", "parallel", "arbitrary"),
        name="ffn",
    )(x, x, gpre, w_up, w_up, conv_w, conv_w, conv_b, conv_b, w_down, gpost)


def _block_overlap_t(nc, ns):
    cs = np.arange(nc)[None, :] * CMP_STRIDE
    ss = np.arange(ns)[:, None] * SLC_LEN
    ov = np.minimum(cs + CMP_LEN, ss + SLC_LEN) - np.maximum(cs, ss)
    return (np.clip(ov, 0, None) / CMP_STRIDE).astype(np.float32)


def kernel(x, g_mix_pre, g_mix_post, g_ffn_pre, g_ffn_post, w_in, cmp_pos, cmp_k_w1, cmp_k_b1, cmp_k_w2,
           cmp_v_w1, cmp_v_b1, cmp_v_w2, sinks, w_out, w_up, conv_w, conv_b, w_down):
    n, T, _ = x.shape
    depth = w_in.shape[0]
    nch = T // CMP_STRIDE
    assert T % (BLOCK_Q * B_PAIRS[-1][1]) == 0 and T % TM_FFN == 0 and nch % BLOCK_Q == 0

    def cols(i):
        return w_in[:, :, IN_OFFS[i]:IN_OFFS[i + 1]]

    (aq, akc, avc, aks, avs, akw, avw, agate, bq, bk, bv, cq, ck, cv) = [cols(i) for i in range(len(IN_SIZES))]
    wz = jnp.concatenate([aq, cq, akc, avc, aks, akw, ck, cv, bq, bk, bv], axis=-1).astype(BF16)
    wtv = jnp.concatenate([avs, avw], axis=-1).transpose(0, 2, 1).astype(BF16)
    wtg = jnp.pad(agate, ((0, 0), (0, 0), (0, GATE_ROWS - agate.shape[-1]))).transpose(0, 2, 1).astype(BF16)
    pos = jnp.broadcast_to(cmp_pos.reshape(depth, 1, CMP_LEN * HEAD_DIM), (depth, 8, CMP_LEN * HEAD_DIM)).astype(BF16)
    w1k, w1v = cmp_k_w1.astype(BF16), cmp_v_w1.astype(BF16)
    w2k = cmp_k_w2.astype(BF16)
    w2vt = cmp_v_w2.transpose(0, 2, 1).astype(BF16)
    w_out_b, w_up_b, w_down_b = w_out.astype(BF16), w_up.astype(BF16), w_down.astype(BF16)
    ovt = jnp.asarray(_block_overlap_t(nch, T // SLC_LEN), BF16)

    row = lambda a: a.reshape(1, -1)
    for l in range(depth):
        z, vt, gt = _in_proj(x, row(g_mix_pre[l]), wz[l], wtv[l], wtg[l])
        xkv = (z[:, :, Z_AKC:Z_AKC + 2 * A_KV * HEAD_DIM]
               .reshape(n, nch, CMP_STRIDE, 2 * A_KV, HEAD_DIM)
               .transpose(0, 3, 1, 2, 4)
               .reshape(n, 2 * A_KV, nch, CMP_STRIDE * HEAD_DIM))
        kc, vct = _compress(xkv, pos[l], w1k[l], row(cmp_k_b1[l]), w2k[l], w1v[l], row(cmp_v_b1[l]), w2vt[l])
        oat = _nsa(z, kc, vct, vt, gt, ovt)
        oc = _swa_sink(z, sinks[l])
        pairs = [_dilated_pair(z, p) for p in range(len(B_PAIRS))]
        x = _out_proj(x, oat, [o for o, _ in pairs], [s for _, s in pairs], oc, w_out_b[l], row(g_mix_post[l]))
        x = _ffn(x, row(g_ffn_pre[l]), w_up_b[l], conv_w[l], row(conv_b[l]), w_down_b[l], row(g_ffn_post[l]))
    return x
```

```python
import functools

import numpy as np
import jax
import jax.numpy as jnp
from jax import lax
from jax.experimental import pallas as pl
from jax.experimental.pallas import tpu as pltpu

D_MODEL = 1024
HEAD_DIM = 64
BLOCK_Q = 128
A_KV, A_REP = 2, 3
A_HEADS = A_KV * A_REP
CMP_LEN, CMP_STRIDE, CMP_HIDDEN = 32, 16, 256
SLC_LEN, SLC_TOPN = 64, 8
WIN_A = 512
B_PAIRS = ((128, 1), (512, 4), (2048, 16))
B_HEADS = 4
C_KV, C_REP = 2, 3
C_HEADS = C_KV * C_REP
WIN_C = 128
D_FF = 2816
CONV_W = 3
EPS = 1e-6

N_ALIBI = C_HEADS + len(B_PAIRS) * B_HEADS + A_HEADS
B_SLOPE_OFF = C_HEADS
A_SLOPE_OFF = C_HEADS + len(B_PAIRS) * B_HEADS
IN_SIZES = ((A_HEADS * HEAD_DIM,) + (A_KV * HEAD_DIM,) * 6 + (A_HEADS * 3,)
            + (len(B_PAIRS) * B_HEADS * HEAD_DIM,) * 3
            + (C_HEADS * HEAD_DIM, C_KV * HEAD_DIM, C_KV * HEAD_DIM))
IN_OFFS = tuple(int(v) for v in np.cumsum((0,) + IN_SIZES))

_SLOPES = tuple(float(v) for v in
                np.exp2(-8.0 * np.arange(1, N_ALIBI + 1, dtype=np.float32) / N_ALIBI).astype(np.float32))

Z_AQ, Z_CQ = 0, 384
Z_AKC, Z_AVC, Z_AKS, Z_AKW, Z_CK, Z_CV = 768, 896, 1024, 1152, 1280, 1408
Z_BQ, Z_BK, Z_BV = 1536, 2304, 3072
Z_W = 3840
Z_CHUNK = 768
GATE_ROWS = 32

F32 = jnp.float32
BF16 = jnp.bfloat16
NEG_INF = float("-inf")

TM_PROJ = 512
TM_FFN = 1024
TF_FFN = 256
FFN_HALO = 16
SLC_CHUNK = 512
VT_ROWS = 80
VMEM_LIMIT = 56 * 1024 * 1024


def _dot(a, b):
    return jnp.dot(a, b, preferred_element_type=F32)


def _dot_nt(a, b):
    return lax.dot_general(a, b, (((1,), (1,)), ((), ())), preferred_element_type=F32)


def _rmsnorm(x, g):
    return x * lax.rsqrt(jnp.mean(x * x, axis=-1, keepdims=True) + EPS) * g


def _gelu_tanh(x):
    return 0.5 * x * (1.0 + jnp.tanh(0.7978845608028654 * (x + 0.044715 * (x * x * x))))


def _params(*sem):
    return pltpu.CompilerParams(dimension_semantics=sem, vmem_limit_bytes=VMEM_LIMIT)


def _in_proj_kernel(x_ref, g_ref, w_ref, wtv_ref, wtg_ref, z_ref, vt_ref, gt_ref):
    h = _rmsnorm(x_ref[0], g_ref[...]).astype(BF16)
    for c in range(0, Z_W, Z_CHUNK):
        z_ref[0, :, c:c + Z_CHUNK] = _dot(h, w_ref[:, c:c + Z_CHUNK]).astype(BF16)
    vt = _dot_nt(wtv_ref[...], h).astype(BF16)
    fill = VT_ROWS - HEAD_DIM
    ones_row = (lax.broadcasted_iota(jnp.int32, (fill, vt.shape[1]), 0) == 0).astype(BF16)
    for i in range(2 * A_KV):
        vt_ref[0, i * VT_ROWS:i * VT_ROWS + HEAD_DIM, :] = vt[i * HEAD_DIM:(i + 1) * HEAD_DIM]
        vt_ref[0, i * VT_ROWS + HEAD_DIM:(i + 1) * VT_ROWS, :] = ones_row
    gt_ref[0] = _dot_nt(wtg_ref[...], h)


def _in_proj(x, g, wz, wtv, wtg):
    n, T, D = x.shape
    tm = TM_PROJ
    return pl.pallas_call(
        _in_proj_kernel,
        grid=(n, T // tm),
        in_specs=[
            pl.BlockSpec((1, tm, D), lambda b, i: (b, i, 0)),
            pl.BlockSpec((1, D), lambda b, i: (0, 0)),
            pl.BlockSpec((D, Z_W), lambda b, i: (0, 0)),
            pl.BlockSpec((4 * HEAD_DIM, D), lambda b, i: (0, 0)),
            pl.BlockSpec((GATE_ROWS, D), lambda b, i: (0, 0)),
        ],
        out_specs=[
            pl.BlockSpec((1, tm, Z_W), lambda b, i: (b, i, 0)),
            pl.BlockSpec((1, 2 * A_KV * VT_ROWS, tm), lambda b, i: (b, 0, i)),
            pl.BlockSpec((1, GATE_ROWS, tm), lambda b, i: (b, 0, i)),
        ],
        out_shape=[
            jax.ShapeDtypeStruct((n, T, Z_W), BF16),
            jax.ShapeDtypeStruct((n, 2 * A_KV * VT_ROWS, T), BF16),
            jax.ShapeDtypeStruct((n, GATE_ROWS, T), F32),
        ],
        compiler_params=_params("parallel", "parallel"),
        name="in_proj",
    )(x, g, wz, wtv, wtg)


def _compress_kernel(xk_ref, xv_ref, pos_ref, w1k_ref, b1k_ref, w2k_ref,
                     w1v_ref, b1v_ref, w2vt_ref, kc_ref, vct_ref):
    half = CMP_STRIDE * HEAD_DIM

    def hidden(x, w1_ref, b1_ref):
        y1 = _dot(x, w1_ref[0:half, :])
        y2 = _dot(x, w1_ref[half:2 * half, :])
        bias = _dot(pos_ref[...], w1_ref[...])[0:1] + b1_ref[...]
        nrow = y2.shape[0]
        return _gelu_tanh(y1 + pltpu.roll(y2, nrow - 1, 0) + bias).astype(BF16)

    kc_ref[0, 0] = _dot(hidden(xk_ref[0, 0], w1k_ref, b1k_ref), w2k_ref[...]).astype(BF16)
    vct_ref[0, 0] = _dot_nt(w2vt_ref[...], hidden(xv_ref[0, 0], w1v_ref, b1v_ref)).astype(BF16)


def _compress(xkv, pos, w1k, b1k, w2k, w1v, b1v, w2vt):
    n, _, nch, kdim = xkv.shape
    full = lambda shape: pl.BlockSpec(shape, lambda b, g: (0,) * len(shape))
    return pl.pallas_call(
        _compress_kernel,
        grid=(n, A_KV),
        in_specs=[
            pl.BlockSpec((1, 1, nch, kdim), lambda b, g: (b, g, 0, 0)),
            pl.BlockSpec((1, 1, nch, kdim), lambda b, g: (b, A_KV + g, 0, 0)),
            full(pos.shape), full(w1k.shape), full(b1k.shape), full(w2k.shape),
            full(w1v.shape), full(b1v.shape), full(w2vt.shape),
        ],
        out_specs=[
            pl.BlockSpec((1, 1, nch, HEAD_DIM), lambda b, g: (b, g, 0, 0)),
            pl.BlockSpec((1, 1, HEAD_DIM, nch), lambda b, g: (b, g, 0, 0)),
        ],
        out_shape=[
            jax.ShapeDtypeStruct((n, A_KV, nch, HEAD_DIM), BF16),
            jax.ShapeDtypeStruct((n, A_KV, HEAD_DIM, nch), BF16),
        ],
        compiler_params=_params("parallel", "parallel"),
        name="compress",
    )(xkv, xkv, pos, w1k, b1k, w2k, w1v, b1v, w2vt)


def _nsa_tables(T):
    sl = np.asarray(_SLOPES[A_SLOPE_OFF:A_SLOPE_OFF + A_HEADS], np.float32).reshape(A_KV, A_REP)
    slope = np.repeat(sl, BLOCK_Q, axis=1)[:, None, :]
    tl = np.tile(np.arange(BLOCK_Q, dtype=np.float32), A_REP)[None, None, :]
    rows = lambda n: np.arange(n, dtype=np.float32)[None, :, None]
    neg = np.float32(NEG_INF)
    p0 = slope * rows(SLC_LEN)
    pj = slope * (rows(T // SLC_LEN) * SLC_LEN)
    pc = slope * (rows(T // CMP_STRIDE) * CMP_STRIDE + np.float32((CMP_LEN - 1) / 2))
    kl = rows(BLOCK_Q)
    pd = np.where(kl <= tl, slope * (kl % SLC_LEN), neg)
    npb = -(-(WIN_A - 1) // BLOCK_Q)
    d = npb * BLOCK_Q + tl - rows((2 * npb + 1) * BLOCK_Q)
    bwin = np.where((d >= 0) & (d <= WIN_A - 1), -slope * d, neg)
    return [jnp.asarray(a.astype(np.float32)) for a in (p0, pj, pc, pd, bwin)]


def _nsa_kernel(q_ref, kc_ref, vct_ref, ks_ref, kw_ref, vt_ref, gt_ref, ovt_ref,
                p0_ref, pj_ref, pc_ref, pd_ref, bwin_ref, o_ref,
                seln_ref, selp_ref, s_ref, p_ref, acc_ref, m_ref):
    qb = pl.program_id(1)
    t0 = pl.multiple_of(qb * BLOCK_Q, BLOCK_Q)
    nl = A_REP * BLOCK_Q
    T = ks_ref.shape[1]
    ncmp = kc_ref.shape[2]
    nsel = ovt_ref.shape[0]
    lane = lax.broadcasted_iota(jnp.int32, (1, nl), 1)
    t_row = t0 + (lane & (BLOCK_Q - 1))
    sig = jax.nn.sigmoid(gt_ref[0])
    q = q_ref[0]
    blocks_per_q = BLOCK_Q // SLC_LEN

    for g in range(A_KV):
        heads = [g * A_REP + r for r in range(A_REP)]
        q3 = jnp.concatenate([q[:, h * HEAD_DIM:(h + 1) * HEAD_DIM] for h in heads], axis=0) * 0.125
        kcols = slice(g * HEAD_DIM, (g + 1) * HEAD_DIM)
        vs_rows = slice(g * VT_ROWS, (g + 1) * VT_ROWS)
        vw_rows = slice((A_KV + g) * VT_ROWS, (A_KV + g + 1) * VT_ROWS)

        st = _dot_nt(kc_ref[0, g], q3) + pc_ref[g]
        cst = lax.broadcasted_iota(jnp.int32, (ncmp, nl), 0) * CMP_STRIDE
        st = jnp.where((cst + (CMP_LEN - 1)) <= t_row, st, NEG_INF)
        m = jnp.max(st, axis=0, keepdims=True)
        m = jnp.where(m == NEG_INF, 0.0, m)
        p = jnp.exp(st - m)
        den = jnp.sum(p, axis=0, keepdims=True)
        pc = p / jnp.where(den > 0, den, 1.0)
        o_cmp = _dot(vct_ref[0, g], pc.astype(BF16))
        psum = pc[:, 0:BLOCK_Q] + pc[:, BLOCK_Q:2 * BLOCK_Q] + pc[:, 2 * BLOCK_Q:3 * BLOCK_Q]
        imp = _dot(ovt_ref[...], psum.astype(BF16))

        j = lax.broadcasted_iota(jnp.int32, (nsel, BLOCK_Q), 0)
        t128 = t0 + lax.broadcasted_iota(jnp.int32, (nsel, BLOCK_Q), 1)
        cur = t128 // SLC_LEN
        forced = (j == 0) | (j == cur) | (j == cur - 1)
        started = j * SLC_LEN <= t128
        score = jnp.where(started, jnp.where(forced, float("inf"), imp), NEG_INF)
        ahead = jnp.zeros((nsel, BLOCK_Q), jnp.int32)
        for k in range(nsel):
            rk = score[k:k + 1, :]
            ahead = ahead + ((rk > score) | ((rk == score) & (j > k))).astype(jnp.int32)
        keep = jnp.where((ahead < SLC_TOPN) & started, 0.0, NEG_INF)
        seln = jnp.concatenate([keep] * A_REP, axis=1) + pj_ref[g]
        seln_ref[...] = seln
        jrow = lax.broadcasted_iota(jnp.int32, (nsel, nl), 0)
        selp_ref[...] = jnp.where(jrow < blocks_per_q * qb, seln, NEG_INF)

        sd = _dot_nt(ks_ref[0, pl.ds(t0, BLOCK_Q), kcols], q3) + pd_ref[g]
        sd = sd + jnp.concatenate(
            [jnp.broadcast_to(seln_ref[pl.ds(blocks_per_q * qb + b, 1), :], (SLC_LEN, nl))
             for b in range(blocks_per_q)], axis=0)
        m_d = jnp.max(sd, axis=0, keepdims=True)
        m_ref[...] = m_d
        acc_ref[...] = jnp.zeros(acc_ref.shape, F32)

        def past(kv, q3=q3, kcols=kcols, vs_rows=vs_rows, m_d=m_d, g=g):
            p0 = p0_ref[g]
            mx = jnp.full((SLC_LEN, nl), NEG_INF, F32)
            for c0 in range(0, kv, SLC_CHUNK):
                s = _dot_nt(ks_ref[0, c0:c0 + SLC_CHUNK, kcols], q3)
                for b in range(SLC_CHUNK // SLC_LEN):
                    jb = c0 // SLC_LEN + b
                    sb = s[b * SLC_LEN:(b + 1) * SLC_LEN] + p0 + selp_ref[jb:jb + 1, :]
                    s_ref[jb * SLC_LEN:(jb + 1) * SLC_LEN, :] = sb
                    mx = jnp.maximum(mx, sb)
            m_all = jnp.maximum(m_d, jnp.max(mx, axis=0, keepdims=True))
            for c0 in range(0, kv, SLC_CHUNK):
                p_ref[c0:c0 + SLC_CHUNK, :] = jnp.exp(s_ref[c0:c0 + SLC_CHUNK, :] - m_all).astype(BF16)
            acc_ref[...] = _dot(vt_ref[0, vs_rows, 0:kv], p_ref[0:kv, :])
            m_ref[...] = m_all

        q_per_chunk = SLC_CHUNK // BLOCK_Q
        for v in range(1, T // SLC_CHUNK + 1):
            pl.when((qb > q_per_chunk * (v - 1)) & (qb <= q_per_chunk * v))(
                functools.partial(past, v * SLC_CHUNK))

        m_all = m_ref[...]
        p_d = jnp.exp(sd - m_all).astype(BF16)
        acc = acc_ref[...] + _dot(vt_ref[0, vs_rows, pl.ds(t0, BLOCK_Q)], p_d)
        o_slc = acc[0:HEAD_DIM] / acc[HEAD_DIM:HEAD_DIM + 1]

        npb = -(-(WIN_A - 1) // BLOCK_Q)
        kwid = (npb + 1) * BLOCK_Q
        kst = pl.multiple_of(jnp.maximum(qb - npb, 0) * BLOCK_Q, BLOCK_Q)
        off = pl.multiple_of(kst + (npb - qb) * BLOCK_Q, BLOCK_Q)
        s = _dot_nt(kw_ref[0, pl.ds(kst, kwid), kcols], q3) + bwin_ref[g, pl.ds(off, kwid), :]
        m = jnp.max(s, axis=0, keepdims=True)
        acc = _dot(vt_ref[0, vw_rows, pl.ds(kst, kwid)], jnp.exp(s - m).astype(BF16))
        o_win = acc[0:HEAD_DIM] / acc[HEAD_DIM:HEAD_DIM + 1]

        for r, h in enumerate(heads):
            cols = slice(r * BLOCK_Q, (r + 1) * BLOCK_Q)
            o = (sig[3 * h:3 * h + 1] * o_cmp[:, cols] + sig[3 * h + 1:3 * h + 2] * o_slc[:, cols]
                 + sig[3 * h + 2:3 * h + 3] * o_win[:, cols])
            o_ref[0, h * HEAD_DIM:(h + 1) * HEAD_DIM, :] = o


def _nsa(z, kc, vct, vt, gt, ovt, tables):
    n, T, _ = z.shape
    ncmp = kc.shape[2]
    nsel = ovt.shape[0]
    kv_w = A_KV * HEAD_DIM
    nl = A_REP * BLOCK_Q
    const = lambda a: pl.BlockSpec(a.shape, lambda b, i: (0,) * a.ndim)
    return pl.pallas_call(
        _nsa_kernel,
        grid=(n, T // BLOCK_Q),
        in_specs=[
            pl.BlockSpec((1, BLOCK_Q, A_HEADS * HEAD_DIM), lambda b, i: (b, i, Z_AQ // (A_HEADS * HEAD_DIM))),
            pl.BlockSpec((1, A_KV, ncmp, HEAD_DIM), lambda b, i: (b, 0, 0, 0)),
            pl.BlockSpec((1, A_KV, HEAD_DIM, ncmp), lambda b, i: (b, 0, 0, 0)),
            pl.BlockSpec((1, T, kv_w), lambda b, i: (b, 0, Z_AKS // kv_w)),
            pl.BlockSpec((1, T, kv_w), lambda b, i: (b, 0, Z_AKW // kv_w)),
            pl.BlockSpec((1, 2 * A_KV * VT_ROWS, T), lambda b, i: (b, 0, 0)),
            pl.BlockSpec((1, GATE_ROWS, BLOCK_Q), lambda b, i: (b, 0, i)),
            const(ovt),
        ] + [const(a) for a in tables],
        out_specs=pl.BlockSpec((1, A_HEADS * HEAD_DIM, BLOCK_Q), lambda b, i: (b, 0, i)),
        out_shape=jax.ShapeDtypeStruct((n, A_HEADS * HEAD_DIM, T), F32),
        scratch_shapes=[
            pltpu.VMEM((nsel, nl), F32), pltpu.VMEM((nsel, nl), F32),
            pltpu.VMEM((T, nl), F32), pltpu.VMEM((T, nl), BF16),
            pltpu.VMEM((VT_ROWS, nl), F32), pltpu.VMEM((1, nl), F32),
        ],
        compiler_params=_params("parallel", "parallel"),
        name="nsa",
    )(z, kc, vct, z, z, vt, gt, ovt, *tables)


def _banded_kernel(*refs, groups, rep, past, kwid, slopes, dist_scale, has_sinks, want_lse, n_axes):
    refs = list(refs)
    sink_ref = refs.pop(0) if has_sinks else None
    q_ref, k_ref, v_ref, o_ref = refs[:4]
    lse_ref = refs[4] if want_lse else None
    qb = pl.program_id(n_axes - 1)
    npb = -(-past // BLOCK_Q)
    kst = pl.multiple_of(jnp.maximum(qb - npb, 0) * BLOCK_Q, BLOCK_Q)
    diff = ((qb * BLOCK_Q - kst) + lax.broadcasted_iota(jnp.int32, (BLOCK_Q, kwid), 0)
            - lax.broadcasted_iota(jnp.int32, (BLOCK_Q, kwid), 1))
    mask = (diff >= 0) & (diff <= past)
    dfl = (diff * dist_scale).astype(F32)
    q = q_ref[0]
    for g in range(groups):
        heads = [g * rep + r for r in range(rep)]
        kwin = k_ref[0, pl.ds(kst, kwid), g * HEAD_DIM:(g + 1) * HEAD_DIM]
        vwin = v_ref[0, pl.ds(kst, kwid), g * HEAD_DIM:(g + 1) * HEAD_DIM]
        qg = jnp.concatenate([q[:, h * HEAD_DIM:(h + 1) * HEAD_DIM] for h in heads], axis=0) * 0.125
        s = _dot_nt(qg, kwin)
        ps, dens, ms = [], [], []
        for r, h in enumerate(heads):
            sr = jnp.where(mask, s[r * BLOCK_Q:(r + 1) * BLOCK_Q] - slopes[h] * dfl, NEG_INF)
            m = jnp.max(sr, axis=1, keepdims=True)
            if has_sinks:
                m = jnp.maximum(m, sink_ref[h])
            p = jnp.exp(sr - m)
            den = jnp.sum(p, axis=1, keepdims=True)
            if has_sinks:
                den = den + jnp.exp(sink_ref[h] - m)
            ps.append(p.astype(BF16))
            dens.append(den)
            ms.append(m)
        o = _dot(jnp.concatenate(ps, axis=0), vwin)
        for r, h in enumerate(heads):
            cols = slice(h * HEAD_DIM, (h + 1) * HEAD_DIM)
            o_ref[0, :, cols] = (o[r * BLOCK_Q:(r + 1) * BLOCK_Q] / dens[r]).astype(o_ref.dtype)
            if want_lse:
                lse_ref[0, :, cols] = jnp.broadcast_to(ms[r] + jnp.log(dens[r]), (BLOCK_Q, HEAD_DIM))


def _swa_sink(z, sinks):
    n, T, _ = z.shape
    qw, kw = C_HEADS * HEAD_DIM, C_KV * HEAD_DIM
    past = WIN_C - 1
    kwid = min((-(-past // BLOCK_Q) + 1) * BLOCK_Q, T)
    kern = functools.partial(
        _banded_kernel, groups=C_KV, rep=C_REP, past=past, kwid=kwid, slopes=_SLOPES[:C_HEADS],
        dist_scale=1, has_sinks=True, want_lse=False, n_axes=2)
    return pl.pallas_call(
        kern,
        grid=(n, T // BLOCK_Q),
        in_specs=[
            pl.BlockSpec(memory_space=pltpu.SMEM),
            pl.BlockSpec((1, BLOCK_Q, qw), lambda b, i: (b, i, Z_CQ // qw)),
            pl.BlockSpec((1, T, kw), lambda b, i: (b, 0, Z_CK // kw)),
            pl.BlockSpec((1, T, kw), lambda b, i: (b, 0, Z_CV // kw)),
        ],
        out_specs=pl.BlockSpec((1, BLOCK_Q, qw), lambda b, i: (b, i, 0)),
        out_shape=jax.ShapeDtypeStruct((n, T, qw), BF16),
        compiler_params=_params("parallel", "parallel"),
        name="swa_sink",
    )(sinks, z, z, z)


def _dilated_pair(z, pair):
    n, T, _ = z.shape
    w, r = B_PAIRS[pair]
    tf = T // r
    bw = B_HEADS * HEAD_DIM
    zf = z.reshape(n, tf, r * Z_W)
    per_row = Z_W // bw
    past = w // r
    kwid = min((-(-past // BLOCK_Q) + 1) * BLOCK_Q, tf)
    hs = B_SLOPE_OFF + pair * B_HEADS
    kern = functools.partial(
        _banded_kernel, groups=B_HEADS, rep=1, past=past, kwid=kwid, slopes=_SLOPES[hs:hs + B_HEADS],
        dist_scale=r, has_sinks=False, want_lse=True, n_axes=3)
    o, lse = pl.pallas_call(
        kern,
        grid=(n, r, tf // BLOCK_Q),
        in_specs=[
            pl.BlockSpec((1, BLOCK_Q, bw), lambda b, j, i: (b, i, j * per_row + Z_BQ // bw + pair)),
            pl.BlockSpec((1, tf, bw), lambda b, j, i: (b, 0, j * per_row + Z_BK // bw + pair)),
            pl.BlockSpec((1, tf, bw), lambda b, j, i: (b, 0, j * per_row + Z_BV // bw + pair)),
        ],
        out_specs=[
            pl.BlockSpec((1, BLOCK_Q, bw), lambda b, j, i: (b, i, j)),
            pl.BlockSpec((1, BLOCK_Q, bw), lambda b, j, i: (b, i, j)),
        ],
        out_shape=[
            jax.ShapeDtypeStruct((n, tf, r * bw), F32),
            jax.ShapeDtypeStruct((n, tf, r * bw), F32),
        ],
        compiler_params=_params("parallel", "parallel", "parallel"),
        name=f"dilated_{pair}",
    )(zf, zf, zf)
    return o.reshape(n, T, bw), lse.reshape(n, T, bw)


def _out_proj_kernel(x_ref, oat_ref, ob0_ref, ob1_ref, ob2_ref, l0_ref, l1_ref, l2_ref,
                     oc_ref, w_ref, g_ref, y_ref):
    l0, l1, l2 = l0_ref[0], l1_ref[0], l2_ref[0]
    mx = jnp.maximum(jnp.maximum(l0, l1), l2)
    e0, e1, e2 = jnp.exp(l0 - mx), jnp.exp(l1 - mx), jnp.exp(l2 - mx)
    ob = (e0 * ob0_ref[0] + e1 * ob1_ref[0] + e2 * ob2_ref[0]) / (e0 + e1 + e2)
    wa = A_HEADS * HEAD_DIM
    wb = wa + B_HEADS * HEAD_DIM
    oa = oat_ref[0].T.astype(BF16)
    mix = (_dot(oa, w_ref[0:wa, :]) + _dot(ob.astype(BF16), w_ref[wa:wb, :])
           + _dot(oc_ref[0], w_ref[wb:, :]))
    y_ref[0] = x_ref[0] + _rmsnorm(mix, g_ref[...])


def _out_proj(x, oat, obs, lses, oc, w, g):
    n, T, D = x.shape
    tm = TM_PROJ
    wa, wb, wc = A_HEADS * HEAD_DIM, B_HEADS * HEAD_DIM, C_HEADS * HEAD_DIM
    row = lambda width: pl.BlockSpec((1, tm, width), lambda b, i: (b, i, 0))
    return pl.pallas_call(
        _out_proj_kernel,
        grid=(n, T // tm),
        in_specs=[
            row(D),
            pl.BlockSpec((1, wa, tm), lambda b, i: (b, 0, i)),
            row(wb), row(wb), row(wb), row(wb), row(wb), row(wb),
            row(wc),
            pl.BlockSpec((D, D), lambda b, i: (0, 0)),
            pl.BlockSpec((1, D), lambda b, i: (0, 0)),
        ],
        out_specs=row(D),
        out_shape=jax.ShapeDtypeStruct((n, T, D), F32),
        compiler_params=_params("parallel", "parallel"),
        name="out_proj",
    )(x, oat, *obs, *lses, oc, w, g)


def _ffn_kernel(x_ref, xp_ref, gpre_ref, wua_ref, wub_ref, cwa_ref, cwb_ref, cba_ref, cbb_ref,
                wd_ref, gpost_ref, y_ref, h_ref, act_ref):
    ti = pl.program_id(1)
    j = pl.program_id(2)
    tf = wua_ref.shape[1]

    @pl.when(j == 0)
    def _():
        hp = _rmsnorm(xp_ref[0], gpre_ref[...])
        h_ref[0:FFN_HALO, :] = jnp.where(ti > 0, hp, 0.0).astype(BF16)
        h_ref[FFN_HALO:, :] = _rmsnorm(x_ref[0], gpre_ref[...]).astype(BF16)

    h = h_ref[...]

    def conv_branch(w_ref, cw_ref, cb_ref):
        u = _dot(h, w_ref[...])
        cw = cw_ref[...]
        y = cw[2:3] * u + cw[1:2] * pltpu.roll(u, 1, 0) + cw[0:1] * pltpu.roll(u, 2, 0)
        return y[FFN_HALO:] + cb_ref[...]

    a = conv_branch(wua_ref, cwa_ref, cba_ref)
    b = conv_branch(wub_ref, cwb_ref, cbb_ref)
    act_ref[:, pl.ds(pl.multiple_of(j * tf, tf), tf)] = (_gelu_tanh(a) * b).astype(BF16)

    @pl.when(j == pl.num_programs(2) - 1)
    def _():
        y_ref[0] = x_ref[0] + _rmsnorm(_dot(act_ref[...], wd_ref[...]), gpost_ref[...])


def _ffn(x, gpre, w_up, conv_w, conv_b, w_down, gpost):
    n, T, D = x.shape
    tm, tf = TM_FFN, TF_FFN
    nff = D_FF // tf
    halo_blocks = tm // FFN_HALO
    return pl.pallas_call(
        _ffn_kernel,
        grid=(n, T // tm, nff),
        in_specs=[
            pl.BlockSpec((1, tm, D), lambda b, i, j: (b, i, 0)),
            pl.BlockSpec((1, FFN_HALO, D), lambda b, i, j: (b, jnp.maximum(i * halo_blocks - 1, 0), 0)),
            pl.BlockSpec((1, D), lambda b, i, j: (0, 0)),
            pl.BlockSpec((D, tf), lambda b, i, j: (0, j)),
            pl.BlockSpec((D, tf), lambda b, i, j: (0, nff + j)),
            pl.BlockSpec((CONV_W, tf), lambda b, i, j: (0, j)),
            pl.BlockSpec((CONV_W, tf), lambda b, i, j: (0, nff + j)),
            pl.BlockSpec((1, tf), lambda b, i, j: (0, j)),
            pl.BlockSpec((1, tf), lambda b, i, j: (0, nff + j)),
            pl.BlockSpec((D_FF, D), lambda b, i, j: (0, 0)),
            pl.BlockSpec((1, D), lambda b, i, j: (0, 0)),
        ],
        out_specs=pl.BlockSpec((1, tm, D), lambda b, i, j: (b, i, 0)),
        out_shape=jax.ShapeDtypeStruct((n, T, D), F32),
        scratch_shapes=[pltpu.VMEM((FFN_HALO + tm, D), BF16), pltpu.VMEM((tm, D_FF), BF16)],
        compiler_params=_params("parallel", "parallel", "arbitrary"),
        name="ffn",
    )(x, x, gpre, w_up, w_up, conv_w, conv_w, conv_b, conv_b, w_down, gpost)


def _block_overlap_t(nc, ns):
    cs = np.arange(nc)[None, :] * CMP_STRIDE
    ss = np.arange(ns)[:, None] * SLC_LEN
    ov = np.minimum(cs + CMP_LEN, ss + SLC_LEN) - np.maximum(cs, ss)
    return (np.clip(ov, 0, None) / CMP_STRIDE).astype(np.float32)


def kernel(x, g_mix_pre, g_mix_post, g_ffn_pre, g_ffn_post, w_in, cmp_pos, cmp_k_w1, cmp_k_b1, cmp_k_w2,
           cmp_v_w1, cmp_v_b1, cmp_v_w2, sinks, w_out, w_up, conv_w, conv_b, w_down):
    n, T, _ = x.shape
    depth = w_in.shape[0]
    nch = T // CMP_STRIDE
    assert T % (BLOCK_Q * B_PAIRS[-1][1]) == 0 and T % TM_FFN == 0 and nch % BLOCK_Q == 0

    def cols(i):
        return w_in[:, :, IN_OFFS[i]:IN_OFFS[i + 1]]

    (aq, akc, avc, aks, avs, akw, avw, agate, bq, bk, bv, cq, ck, cv) = [cols(i) for i in range(len(IN_SIZES))]
    wz = jnp.concatenate([aq, cq, akc, avc, aks, akw, ck, cv, bq, bk, bv], axis=-1).astype(BF16)
    wtv = jnp.concatenate([avs, avw], axis=-1).transpose(0, 2, 1).astype(BF16)
    wtg = jnp.pad(agate, ((0, 0), (0, 0), (0, GATE_ROWS - agate.shape[-1]))).transpose(0, 2, 1).astype(BF16)
    pos = jnp.broadcast_to(cmp_pos.reshape(depth, 1, CMP_LEN * HEAD_DIM), (depth, 8, CMP_LEN * HEAD_DIM)).astype(BF16)
    w1k, w1v = cmp_k_w1.astype(BF16), cmp_v_w1.astype(BF16)
    w2k = cmp_k_w2.astype(BF16)
    w2vt = cmp_v_w2.transpose(0, 2, 1).astype(BF16)
    w_out_b, w_up_b, w_down_b = w_out.astype(BF16), w_up.astype(BF16), w_down.astype(BF16)
    ovt = jnp.asarray(_block_overlap_t(nch, T // SLC_LEN), BF16)

    tables = _nsa_tables(T)
    row = lambda a: a.reshape(1, -1)
    for l in range(depth):
        z, vt, gt = _in_proj(x, row(g_mix_pre[l]), wz[l], wtv[l], wtg[l])
        xkv = (z[:, :, Z_AKC:Z_AKC + 2 * A_KV * HEAD_DIM]
               .reshape(n, nch, CMP_STRIDE, 2 * A_KV, HEAD_DIM)
               .transpose(0, 3, 1, 2, 4)
               .reshape(n, 2 * A_KV, nch, CMP_STRIDE * HEAD_DIM))
        kc, vct = _compress(xkv, pos[l], w1k[l], row(cmp_k_b1[l]), w2k[l], w1v[l], row(cmp_v_b1[l]), w2vt[l])
        oat = _nsa(z, kc, vct, vt, gt, ovt, tables)
        oc = _swa_sink(z, sinks[l])
        pairs = [_dilated_pair(z, p) for p in range(len(B_PAIRS))]
        x = _out_proj(x, oat, [o for o, _ in pairs], [s for _, s in pairs], oc, w_out_b[l], row(g_mix_post[l]))
        x = _ffn(x, row(g_ffn_pre[l]), w_up_b[l], conv_w[l], row(conv_b[l]), w_down_b[l], row(g_ffn_post[l]))
    return x
```

```python
import functools

import numpy as np
import jax
import jax.numpy as jnp
from jax import lax
from jax.experimental import pallas as pl
from jax.experimental.pallas import tpu as pltpu

D_MODEL = 1024
HEAD_DIM = 64
BLOCK_Q = 128
A_KV, A_REP = 2, 3
A_HEADS = A_KV * A_REP
CMP_LEN, CMP_STRIDE, CMP_HIDDEN = 32, 16, 256
SLC_LEN, SLC_TOPN = 64, 8
WIN_A = 512
B_PAIRS = ((128, 1), (512, 4), (2048, 16))
B_HEADS = 4
C_KV, C_REP = 2, 3
C_HEADS = C_KV * C_REP
WIN_C = 128
D_FF = 2816
CONV_W = 3
EPS = 1e-6

N_ALIBI = C_HEADS + len(B_PAIRS) * B_HEADS + A_HEADS
B_SLOPE_OFF = C_HEADS
A_SLOPE_OFF = C_HEADS + len(B_PAIRS) * B_HEADS
IN_SIZES = ((A_HEADS * HEAD_DIM,) + (A_KV * HEAD_DIM,) * 6 + (A_HEADS * 3,)
            + (len(B_PAIRS) * B_HEADS * HEAD_DIM,) * 3
            + (C_HEADS * HEAD_DIM, C_KV * HEAD_DIM, C_KV * HEAD_DIM))
IN_OFFS = tuple(int(v) for v in np.cumsum((0,) + IN_SIZES))

_SLOPES = tuple(float(v) for v in
                np.exp2(-8.0 * np.arange(1, N_ALIBI + 1, dtype=np.float32) / N_ALIBI).astype(np.float32))

Z_AQ, Z_CQ = 0, 384
Z_AKC, Z_AVC, Z_AKS, Z_AKW, Z_CK = 768, 896, 1024, 1152, 1280
ZA_W = 1408
W_CV = 1408
W_B = 1536
B_W = B_HEADS * HEAD_DIM
PAIR_W = 3 * B_W
PAIR_OUT_W = 2 * B_W + 2 * B_W
W_ALL = W_B + len(B_PAIRS) * PAIR_W
GATE_ROWS = 32
C_SLAB_HEADS = tuple((r, C_REP + r) for r in range(C_REP))
C_HEAD_ORDER = tuple(h for pair in C_SLAB_HEADS for h in pair)

F32 = jnp.float32
BF16 = jnp.bfloat16
NEG_INF = float("-inf")

TM_PROJ = 512
TM_FFN = 1024
TF_FFN = 256
FFN_HALO = 16
FFN_ROW_CHUNKS = 4
BAND_QB = 4
SLC_CHUNK = 512
VT_ROWS = 80
VMEM_LIMIT = 56 * 1024 * 1024


def _dot(a, b):
    return jnp.dot(a, b, preferred_element_type=F32)


def _dot_nt(a, b):
    return lax.dot_general(a, b, (((1,), (1,)), ((), ())), preferred_element_type=F32)


def _rmsnorm(x, g):
    return x * lax.rsqrt(jnp.mean(x * x, axis=-1, keepdims=True) + EPS) * g


def _gelu_tanh(x):
    return 0.5 * x * (1.0 + jnp.tanh(0.7978845608028654 * (x + 0.044715 * (x * x * x))))


def _params(*sem):
    return pltpu.CompilerParams(dimension_semantics=sem, vmem_limit_bytes=VMEM_LIMIT)


def _pad_values(v):
    upper = lax.broadcasted_iota(jnp.int32, (v.shape[0], 128), 1) >= HEAD_DIM
    parts = []
    for s in range(v.shape[1] // 128):
        slab = v[:, s * 128:(s + 1) * 128].astype(BF16)
        parts += [jnp.where(upper, jnp.zeros_like(slab), slab), jnp.where(upper, slab, jnp.zeros_like(slab))]
    return jnp.concatenate(parts, axis=1)


def _in_proj_kernel(x_ref, g_ref, w_ref, wtv_ref, wtg_ref, za_ref, cv_ref, b0_ref, b1_ref, b2_ref,
                    vt_ref, gt_ref, slab_ref):
    h = _rmsnorm(x_ref[0], g_ref[...]).astype(BF16)
    tm = h.shape[0]
    half = ZA_W // 2
    for c in range(0, ZA_W, half):
        za_ref[0, :, c:c + half] = _dot(h, w_ref[:, c:c + half]).astype(BF16)
    cv_ref[0, 0] = _pad_values(_dot(h, w_ref[:, W_CV:W_B]))
    n_slab = PAIR_W // 128
    for p, b_ref in enumerate((b0_ref, b1_ref, b2_ref)):
        r = B_PAIRS[p][1]
        res = _dot(h, w_ref[:, W_B + p * PAIR_W:W_B + (p + 1) * PAIR_W])
        if r == 1:
            b_ref[0, 0] = jnp.concatenate([res[:, 0:2 * B_W].astype(BF16), _pad_values(res[:, 2 * B_W:])], axis=1)
            continue
        for c in range(n_slab):
            slab_ref[c] = res[:, c * 128:(c + 1) * 128]
        for j in range(r):
            rows = jnp.concatenate([slab_ref[c, pl.ds(j, tm // r, stride=r), :] for c in range(n_slab)], axis=1)
            b_ref[0, j] = jnp.concatenate([rows[:, 0:2 * B_W].astype(BF16), _pad_values(rows[:, 2 * B_W:])], axis=1)
    vt = _dot_nt(wtv_ref[...], h).astype(BF16)
    fill = VT_ROWS - HEAD_DIM
    ones_row = (lax.broadcasted_iota(jnp.int32, (fill, vt.shape[1]), 0) == 0).astype(BF16)
    for i in range(2 * A_KV):
        vt_ref[0, i * VT_ROWS:i * VT_ROWS + HEAD_DIM, :] = vt[i * HEAD_DIM:(i + 1) * HEAD_DIM]
        vt_ref[0, i * VT_ROWS + HEAD_DIM:(i + 1) * VT_ROWS, :] = ones_row
    gt_ref[0] = _dot_nt(wtg_ref[...], h)


def _in_proj(x, g, wz, wtv, wtg):
    n, T, D = x.shape
    tm = TM_PROJ
    return pl.pallas_call(
        _in_proj_kernel,
        grid=(n, T // tm),
        in_specs=[
            pl.BlockSpec((1, tm, D), lambda b, i: (b, i, 0)),
            pl.BlockSpec((1, D), lambda b, i: (0, 0)),
            pl.BlockSpec((D, W_ALL), lambda b, i: (0, 0)),
            pl.BlockSpec((4 * HEAD_DIM, D), lambda b, i: (0, 0)),
            pl.BlockSpec((GATE_ROWS, D), lambda b, i: (0, 0)),
        ],
        out_specs=[
            pl.BlockSpec((1, tm, ZA_W), lambda b, i: (b, i, 0)),
            pl.BlockSpec((1, 1, tm, 2 * C_KV * HEAD_DIM), lambda b, i: (b, 0, i, 0)),
        ] + [
            pl.BlockSpec((1, r, tm // r, PAIR_OUT_W), lambda b, i: (b, 0, i, 0)) for _, r in B_PAIRS
        ] + [
            pl.BlockSpec((1, 2 * A_KV * VT_ROWS, tm), lambda b, i: (b, 0, i)),
            pl.BlockSpec((1, GATE_ROWS, tm), lambda b, i: (b, 0, i)),
        ],
        out_shape=[
            jax.ShapeDtypeStruct((n, T, ZA_W), BF16),
            jax.ShapeDtypeStruct((n, 1, T, 2 * C_KV * HEAD_DIM), BF16),
        ] + [
            jax.ShapeDtypeStruct((n, r, T // r, PAIR_OUT_W), BF16) for _, r in B_PAIRS
        ] + [
            jax.ShapeDtypeStruct((n, 2 * A_KV * VT_ROWS, T), BF16),
            jax.ShapeDtypeStruct((n, GATE_ROWS, T), F32),
        ],
        scratch_shapes=[pltpu.VMEM((PAIR_W // 128, tm, 128), F32)],
        compiler_params=_params("parallel", "parallel"),
        name="in_proj",
    )(x, g, wz, wtv, wtg)


def _compress_kernel(xk_ref, xv_ref, pos_ref, w1k_ref, b1k_ref, w2k_ref,
                     w1v_ref, b1v_ref, w2vt_ref, kc_ref, vct_ref):
    half = CMP_STRIDE * HEAD_DIM

    def hidden(x, w1_ref, b1_ref):
        y1 = _dot(x, w1_ref[0:half, :])
        y2 = _dot(x, w1_ref[half:2 * half, :])
        bias = _dot(pos_ref[...], w1_ref[...])[0:1] + b1_ref[...]
        nrow = y2.shape[0]
        return _gelu_tanh(y1 + pltpu.roll(y2, nrow - 1, 0) + bias).astype(BF16)

    kc_ref[0, 0] = _dot(hidden(xk_ref[0, 0], w1k_ref, b1k_ref), w2k_ref[...]).astype(BF16)
    vct_ref[0, 0] = _dot_nt(w2vt_ref[...], hidden(xv_ref[0, 0], w1v_ref, b1v_ref)).astype(BF16)


def _compress(xkv, pos, w1k, b1k, w2k, w1v, b1v, w2vt):
    n, _, nch, kdim = xkv.shape
    full = lambda shape: pl.BlockSpec(shape, lambda b, g: (0,) * len(shape))
    return pl.pallas_call(
        _compress_kernel,
        grid=(n, A_KV),
        in_specs=[
            pl.BlockSpec((1, 1, nch, kdim), lambda b, g: (b, g, 0, 0)),
            pl.BlockSpec((1, 1, nch, kdim), lambda b, g: (b, A_KV + g, 0, 0)),
            full(pos.shape), full(w1k.shape), full(b1k.shape), full(w2k.shape),
            full(w1v.shape), full(b1v.shape), full(w2vt.shape),
        ],
        out_specs=[
            pl.BlockSpec((1, 1, nch, HEAD_DIM), lambda b, g: (b, g, 0, 0)),
            pl.BlockSpec((1, 1, HEAD_DIM, nch), lambda b, g: (b, g, 0, 0)),
        ],
        out_shape=[
            jax.ShapeDtypeStruct((n, A_KV, nch, HEAD_DIM), BF16),
            jax.ShapeDtypeStruct((n, A_KV, HEAD_DIM, nch), BF16),
        ],
        compiler_params=_params("parallel", "parallel"),
        name="compress",
    )(xkv, xkv, pos, w1k, b1k, w2k, w1v, b1v, w2vt)


def _nsa_tables(T):
    sl = np.asarray(_SLOPES[A_SLOPE_OFF:A_SLOPE_OFF + A_HEADS], np.float32).reshape(A_KV, A_REP)
    slope = np.repeat(sl, BLOCK_Q, axis=1)[:, None, :]
    tl = np.tile(np.arange(BLOCK_Q, dtype=np.float32), A_REP)[None, None, :]
    rows = lambda n: np.arange(n, dtype=np.float32)[None, :, None]
    neg = np.float32(NEG_INF)
    p0 = slope * rows(SLC_LEN)
    pj = slope * (rows(T // SLC_LEN) * SLC_LEN)
    pc = slope * (rows(T // CMP_STRIDE) * CMP_STRIDE + np.float32((CMP_LEN - 1) / 2))
    kl = rows(BLOCK_Q)
    pd = np.where(kl <= tl, slope * (kl % SLC_LEN), neg)
    npb = -(-(WIN_A - 1) // BLOCK_Q)
    d = npb * BLOCK_Q + tl - rows((2 * npb + 1) * BLOCK_Q)
    bwin = np.where((d >= 0) & (d <= WIN_A - 1), -slope * d, neg)
    return [jnp.asarray(a.astype(np.float32)) for a in (p0, pj, pc, pd, bwin)]


def _nsa_kernel(q_ref, kc_ref, vct_ref, ks_ref, kw_ref, vt_ref, gt_ref, ovt_ref,
                p0_ref, pj_ref, pc_ref, pd_ref, bwin_ref, o_ref,
                seln_ref, selp_ref, s_ref, p_ref, acc_ref, m_ref):
    qb = pl.program_id(1)
    t0 = pl.multiple_of(qb * BLOCK_Q, BLOCK_Q)
    nl = A_REP * BLOCK_Q
    T = ks_ref.shape[1]
    ncmp = kc_ref.shape[2]
    nsel = ovt_ref.shape[0]
    lane = lax.broadcasted_iota(jnp.int32, (1, nl), 1)
    t_row = t0 + (lane & (BLOCK_Q - 1))
    sig = jax.nn.sigmoid(gt_ref[0])
    q = q_ref[0]
    blocks_per_q = BLOCK_Q // SLC_LEN

    for g in range(A_KV):
        heads = [g * A_REP + r for r in range(A_REP)]
        q3 = jnp.concatenate([q[:, h * HEAD_DIM:(h + 1) * HEAD_DIM] for h in heads], axis=0) * 0.125
        kcols = slice(g * HEAD_DIM, (g + 1) * HEAD_DIM)
        vs_rows = slice(g * VT_ROWS, (g + 1) * VT_ROWS)
        vw_rows = slice((A_KV + g) * VT_ROWS, (A_KV + g + 1) * VT_ROWS)

        st = _dot_nt(kc_ref[0, g], q3) + pc_ref[g]
        cst = lax.broadcasted_iota(jnp.int32, (ncmp, nl), 0) * CMP_STRIDE
        st = jnp.where((cst + (CMP_LEN - 1)) <= t_row, st, NEG_INF)
        m = jnp.max(st, axis=0, keepdims=True)
        m = jnp.where(m == NEG_INF, 0.0, m)
        p = jnp.exp(st - m)
        den = jnp.sum(p, axis=0, keepdims=True)
        pc = p / jnp.where(den > 0, den, 1.0)
        o_cmp = _dot(vct_ref[0, g], pc.astype(BF16))
        psum = pc[:, 0:BLOCK_Q] + pc[:, BLOCK_Q:2 * BLOCK_Q] + pc[:, 2 * BLOCK_Q:3 * BLOCK_Q]
        imp = _dot(ovt_ref[...], psum.astype(BF16))

        j = lax.broadcasted_iota(jnp.int32, (nsel, BLOCK_Q), 0)
        t128 = t0 + lax.broadcasted_iota(jnp.int32, (nsel, BLOCK_Q), 1)
        cur = t128 // SLC_LEN
        forced = (j == 0) | (j == cur) | (j == cur - 1)
        started = j * SLC_LEN <= t128
        score = jnp.where(started, jnp.where(forced, float("inf"), imp), NEG_INF)
        ahead = jnp.zeros((nsel, BLOCK_Q), jnp.int32)
        for k in range(nsel):
            rk = score[k:k + 1, :]
            ahead = ahead + ((rk > score) | ((rk == score) & (j > k))).astype(jnp.int32)
        keep = jnp.where((ahead < SLC_TOPN) & started, 0.0, NEG_INF)
        seln = jnp.concatenate([keep] * A_REP, axis=1) + pj_ref[g]
        seln_ref[...] = seln
        jrow = lax.broadcasted_iota(jnp.int32, (nsel, nl), 0)
        selp_ref[...] = jnp.where(jrow < blocks_per_q * qb, seln, NEG_INF)

        sd = _dot_nt(ks_ref[0, pl.ds(t0, BLOCK_Q), kcols], q3) + pd_ref[g]
        sd = sd + jnp.concatenate(
            [jnp.broadcast_to(seln_ref[pl.ds(blocks_per_q * qb + b, 1), :], (SLC_LEN, nl))
             for b in range(blocks_per_q)], axis=0)
        m_d = jnp.max(sd, axis=0, keepdims=True)
        m_ref[...] = m_d
        acc_ref[...] = jnp.zeros(acc_ref.shape, F32)

        def past(kv, q3=q3, kcols=kcols, vs_rows=vs_rows, m_d=m_d, g=g):
            p0 = p0_ref[g]
            mx = jnp.full((SLC_LEN, nl), NEG_INF, F32)
            for c0 in range(0, kv, SLC_CHUNK):
                s = _dot_nt(ks_ref[0, c0:c0 + SLC_CHUNK, kcols], q3)
                for b in range(SLC_CHUNK // SLC_LEN):
                    jb = c0 // SLC_LEN + b
                    sb = s[b * SLC_LEN:(b + 1) * SLC_LEN] + p0 + selp_ref[jb:jb + 1, :]
                    s_ref[jb * SLC_LEN:(jb + 1) * SLC_LEN, :] = sb
                    mx = jnp.maximum(mx, sb)
            m_all = jnp.maximum(m_d, jnp.max(mx, axis=0, keepdims=True))
            for c0 in range(0, kv, SLC_CHUNK):
                p_ref[c0:c0 + SLC_CHUNK, :] = jnp.exp(s_ref[c0:c0 + SLC_CHUNK, :] - m_all).astype(BF16)
            acc_ref[...] = _dot(vt_ref[0, vs_rows, 0:kv], p_ref[0:kv, :])
            m_ref[...] = m_all

        q_per_chunk = SLC_CHUNK // BLOCK_Q
        for v in range(1, T // SLC_CHUNK + 1):
            pl.when((qb > q_per_chunk * (v - 1)) & (qb <= q_per_chunk * v))(
                functools.partial(past, v * SLC_CHUNK))

        m_all = m_ref[...]
        p_d = jnp.exp(sd - m_all).astype(BF16)
        acc = acc_ref[...] + _dot(vt_ref[0, vs_rows, pl.ds(t0, BLOCK_Q)], p_d)
        o_slc = acc[0:HEAD_DIM] / acc[HEAD_DIM:HEAD_DIM + 1]

        npb = -(-(WIN_A - 1) // BLOCK_Q)
        kwid = (npb + 1) * BLOCK_Q
        kst = pl.multiple_of(jnp.maximum(qb - npb, 0) * BLOCK_Q, BLOCK_Q)
        off = pl.multiple_of(kst + (npb - qb) * BLOCK_Q, BLOCK_Q)
        s = _dot_nt(kw_ref[0, pl.ds(kst, kwid), kcols], q3) + bwin_ref[g, pl.ds(off, kwid), :]
        m = jnp.max(s, axis=0, keepdims=True)
        acc = _dot(vt_ref[0, vw_rows, pl.ds(kst, kwid)], jnp.exp(s - m).astype(BF16))
        o_win = acc[0:HEAD_DIM] / acc[HEAD_DIM:HEAD_DIM + 1]

        for r, h in enumerate(heads):
            cols = slice(r * BLOCK_Q, (r + 1) * BLOCK_Q)
            o = (sig[3 * h:3 * h + 1] * o_cmp[:, cols] + sig[3 * h + 1:3 * h + 2] * o_slc[:, cols]
                 + sig[3 * h + 2:3 * h + 3] * o_win[:, cols])
            o_ref[0, h * HEAD_DIM:(h + 1) * HEAD_DIM, :] = o


def _nsa(z, kc, vct, vt, gt, ovt, tables):
    n, T, _ = z.shape
    ncmp = kc.shape[2]
    nsel = ovt.shape[0]
    kv_w = A_KV * HEAD_DIM
    nl = A_REP * BLOCK_Q
    const = lambda a: pl.BlockSpec(a.shape, lambda b, i: (0,) * a.ndim)
    return pl.pallas_call(
        _nsa_kernel,
        grid=(n, T // BLOCK_Q),
        in_specs=[
            pl.BlockSpec((1, BLOCK_Q, A_HEADS * HEAD_DIM), lambda b, i: (b, i, Z_AQ // (A_HEADS * HEAD_DIM))),
            pl.BlockSpec((1, A_KV, ncmp, HEAD_DIM), lambda b, i: (b, 0, 0, 0)),
            pl.BlockSpec((1, A_KV, HEAD_DIM, ncmp), lambda b, i: (b, 0, 0, 0)),
            pl.BlockSpec((1, T, kv_w), lambda b, i: (b, 0, Z_AKS // kv_w)),
            pl.BlockSpec((1, T, kv_w), lambda b, i: (b, 0, Z_AKW // kv_w)),
            pl.BlockSpec((1, 2 * A_KV * VT_ROWS, T), lambda b, i: (b, 0, 0)),
            pl.BlockSpec((1, GATE_ROWS, BLOCK_Q), lambda b, i: (b, 0, i)),
            const(ovt),
        ] + [const(a) for a in tables],
        out_specs=pl.BlockSpec((1, A_HEADS * HEAD_DIM, BLOCK_Q), lambda b, i: (b, 0, i)),
        out_shape=jax.ShapeDtypeStruct((n, A_HEADS * HEAD_DIM, T), F32),
        scratch_shapes=[
            pltpu.VMEM((nsel, nl), F32), pltpu.VMEM((nsel, nl), F32),
            pltpu.VMEM((T, nl), F32), pltpu.VMEM((T, nl), BF16),
            pltpu.VMEM((VT_ROWS, nl), F32), pltpu.VMEM((1, nl), F32),
        ],
        compiler_params=_params("parallel", "parallel"),
        name="nsa",
    )(z, kc, vct, z, z, vt, gt, ovt, *tables)


def _band_table(slopes, dist_scale, past, kwid):
    npb = -(-past // BLOCK_Q)
    i = np.arange(BLOCK_Q)[:, None]
    j = np.arange(kwid + npb * BLOCK_Q)[None, :]
    diff = npb * BLOCK_Q + i - j
    sl = np.asarray(slopes, np.float32)[:, None, None]
    tab = np.where((diff >= 0) & (diff <= past), -sl * (diff * dist_scale).astype(np.float32), np.float32(NEG_INF))
    return jnp.asarray(tab.astype(np.float32))


def _banded_kernel(*refs, q_slabs, kv_slabs, head_of, past, kwid, n_sub, n_qb, has_sinks, want_lse):
    refs = list(refs)
    sink_ref = refs.pop(0) if has_sinks else None
    tab_ref, q_ref, k_ref, v_ref, o_ref = refs[:5]
    lse_ref = refs[5] if want_lse else None
    npb = -(-past // BLOCK_Q)
    cr = min(BLOCK_Q, 16 * 1024 // kwid)
    upper = lax.broadcasted_iota(jnp.int32, (cr, 128), 1) >= HEAD_DIM
    ones = jnp.ones((kwid, 128), BF16)
    for sub in range(n_sub):
        for part in range(n_qb * BLOCK_Q // cr):
            qb = pl.program_id(2) * n_qb + part * cr // BLOCK_Q
            kst = pl.multiple_of(jnp.maximum(qb - npb, 0) * BLOCK_Q, BLOCK_Q)
            off = pl.multiple_of(kst + (npb - qb) * BLOCK_Q, BLOCK_Q)
            rows = slice(part * cr, (part + 1) * cr)
            trow = slice(part * cr % BLOCK_Q, part * cr % BLOCK_Q + cr)
            for s_out in range(q_slabs):
                lanes = slice(s_out * 128, (s_out + 1) * 128)
                q = q_ref[0, sub, rows, lanes]
                kv = s_out * kv_slabs // q_slabs
                kwin = k_ref[0, sub, pl.ds(kst, kwid), kv * 128:(kv + 1) * 128]
                out, lses = [], []
                for half in range(2):
                    h = head_of[s_out][half]
                    qm = jnp.where(upper == (half == 1), q, jnp.zeros_like(q)) * 0.125
                    sr = _dot_nt(qm, kwin) + tab_ref[h, trow, pl.ds(off, kwid)]
                    m = jnp.max(sr, axis=1, keepdims=True)
                    if has_sinks:
                        m = jnp.maximum(m, sink_ref[h])
                    p = jnp.exp(sr - m).astype(BF16)
                    vpad = v_ref[0, sub, pl.ds(kst, kwid), (2 * kv + half) * 128:(2 * kv + half + 1) * 128]
                    den = _dot(p, ones)
                    if has_sinks:
                        den = den + jnp.exp(sink_ref[h] - m)
                    out.append(_dot(p, vpad) / den)
                    lses.append(m + jnp.log(den))
                o_ref[0, sub, rows, lanes] = (out[0] + out[1]).astype(o_ref.dtype)
                if want_lse:
                    lse_ref[0, sub, rows, lanes] = jnp.where(upper, lses[1], lses[0])


def _banded(q_arr, k_arr, v_arr, q_col, k_col, v_col, *, head_of, kv_heads, past, slopes, dist_scale,
            n_sub, n_qb, sinks, want_lse, out_dtype, name):
    n, n_seq, tf, _ = q_arr.shape
    qw, kw, vw = 128 * len(head_of), kv_heads * HEAD_DIM, 2 * kv_heads * HEAD_DIM
    kwid = min((-(-past // BLOCK_Q) + 1) * BLOCK_Q, tf)
    tab = _band_table(slopes, dist_scale, past, kwid)
    kern = functools.partial(_banded_kernel, q_slabs=len(head_of), kv_slabs=kw // 128, head_of=head_of,
                             past=past, kwid=kwid, n_sub=n_sub, n_qb=n_qb, has_sinks=sinks is not None,
                             want_lse=want_lse)
    rows = n_qb * BLOCK_Q
    out_spec = pl.BlockSpec((1, n_sub, rows, qw), lambda b, j, i: (b, j, i, 0))
    out_shape = jax.ShapeDtypeStruct((n, n_seq, tf, qw), out_dtype)
    in_specs = [
        pl.BlockSpec(tab.shape, lambda b, j, i: (0, 0, 0)),
        pl.BlockSpec((1, n_sub, rows, qw), lambda b, j, i: (b, j, i, q_col)),
        pl.BlockSpec((1, n_sub, tf, kw), lambda b, j, i: (b, j, 0, k_col)),
        pl.BlockSpec((1, n_sub, tf, vw), lambda b, j, i: (b, j, 0, v_col)),
    ]
    args = [tab, q_arr, k_arr, v_arr]
    if sinks is not None:
        in_specs.insert(0, pl.BlockSpec(memory_space=pltpu.SMEM))
        args.insert(0, sinks)
    return pl.pallas_call(
        kern,
        grid=(n, n_seq // n_sub, tf // rows),
        in_specs=in_specs,
        out_specs=[out_spec, out_spec] if want_lse else out_spec,
        out_shape=[out_shape, jax.ShapeDtypeStruct(out_shape.shape, F32)] if want_lse else out_shape,
        compiler_params=_params("parallel", "parallel", "parallel"),
        name=name,
    )(*args)


def _swa_sink(za, cv, sinks):
    n, T, _ = za.shape
    za4 = za.reshape(n, 1, T, ZA_W)
    return _banded(za4, za4, cv, Z_CQ // (C_HEADS * HEAD_DIM), Z_CK // (C_KV * HEAD_DIM), 0,
                   head_of=C_SLAB_HEADS, kv_heads=C_KV, past=WIN_C - 1, slopes=_SLOPES[:C_HEADS], dist_scale=1,
                   n_sub=1, n_qb=BAND_QB, sinks=sinks, want_lse=False, out_dtype=BF16, name="swa_sink")


def _dilated_pair(bp, pair):
    w, r = B_PAIRS[pair]
    tf = bp.shape[2]
    n_qb = min(BAND_QB, tf // BLOCK_Q)
    n_sub = min(r, BAND_QB // n_qb)
    hs = B_SLOPE_OFF + pair * B_HEADS
    head_of = tuple((2 * s, 2 * s + 1) for s in range(B_HEADS // 2))
    return _banded(bp, bp, bp, 0, 1, 1, head_of=head_of, kv_heads=B_HEADS, past=w // r,
                   slopes=_SLOPES[hs:hs + B_HEADS], dist_scale=r, n_sub=n_sub, n_qb=n_qb, sinks=None,
                   want_lse=True, out_dtype=F32, name=f"dilated_{pair}")


def _out_proj_kernel(x_ref, oat_ref, ob0_ref, ob1_ref, ob2_ref, l0_ref, l1_ref, l2_ref,
                     oc_ref, w_ref, g_ref, y_ref, slab_ref):
    tm = x_ref.shape[1]
    n_slab = B_W // 128

    def unfold(ref, slot):
        r = ref.shape[1]
        if r == 1:
            return ref[0, 0]
        for j in range(r):
            for c in range(n_slab):
                slab_ref[slot * n_slab + c, pl.ds(j, tm // r, stride=r), :] = ref[0, j, :, c * 128:(c + 1) * 128]
        return jnp.concatenate([slab_ref[slot * n_slab + c] for c in range(n_slab)], axis=1)

    obs = [unfold(ref, i) for i, ref in enumerate((ob0_ref, ob1_ref, ob2_ref))]
    l0, l1, l2 = [unfold(ref, 3 + i) for i, ref in enumerate((l0_ref, l1_ref, l2_ref))]
    mx = jnp.maximum(jnp.maximum(l0, l1), l2)
    e0, e1, e2 = jnp.exp(l0 - mx), jnp.exp(l1 - mx), jnp.exp(l2 - mx)
    ob = (e0 * obs[0] + e1 * obs[1] + e2 * obs[2]) / (e0 + e1 + e2)
    wa = A_HEADS * HEAD_DIM
    wb = wa + B_HEADS * HEAD_DIM
    oa = oat_ref[0].T.astype(BF16)
    mix = (_dot(oa, w_ref[0:wa, :]) + _dot(ob.astype(BF16), w_ref[wa:wb, :])
           + _dot(oc_ref[0, 0], w_ref[wb:, :]))
    y_ref[0] = x_ref[0] + _rmsnorm(mix, g_ref[...])


def _out_proj(x, oat, obs, lses, oc, w, g):
    n, T, D = x.shape
    tm = TM_PROJ
    wa, wc = A_HEADS * HEAD_DIM, C_HEADS * HEAD_DIM
    row = lambda width: pl.BlockSpec((1, tm, width), lambda b, i: (b, i, 0))
    folded = lambda a: pl.BlockSpec((1, a.shape[1], tm // a.shape[1], a.shape[3]), lambda b, i: (b, 0, i, 0))
    return pl.pallas_call(
        _out_proj_kernel,
        grid=(n, T // tm),
        in_specs=[row(D), pl.BlockSpec((1, wa, tm), lambda b, i: (b, 0, i))]
        + [folded(a) for a in obs] + [folded(a) for a in lses]
        + [folded(oc), pl.BlockSpec((D, D), lambda b, i: (0, 0)), pl.BlockSpec((1, D), lambda b, i: (0, 0))],
        out_specs=row(D),
        out_shape=jax.ShapeDtypeStruct((n, T, D), F32),
        scratch_shapes=[pltpu.VMEM((2 * len(B_PAIRS) * (B_W // 128), tm, 128), F32)],
        compiler_params=_params("parallel", "parallel"),
        name="out_proj",
    )(x, oat, *obs, *lses, oc, w, g)


def _ffn_kernel(x_ref, xp_ref, gpre_ref, wua_ref, wub_ref, cwa_ref, cwb_ref, cba_ref, cbb_ref,
                wd_ref, gpost_ref, y_ref, h_ref, act_ref):
    ti = pl.program_id(1)
    j = pl.program_id(2)
    tf = wua_ref.shape[1]

    @pl.when(j == 0)
    def _():
        hp = _rmsnorm(xp_ref[0], gpre_ref[...])
        h_ref[0:FFN_HALO, :] = jnp.where(ti > 0, hp, 0.0).astype(BF16)
        h_ref[FFN_HALO:, :] = _rmsnorm(x_ref[0], gpre_ref[...]).astype(BF16)

    def conv_branch(h, w_ref, cw_ref, cb_ref):
        u = _dot(h, w_ref[...])
        cw = cw_ref[...]
        y = cw[2:3] * u + cw[1:2] * pltpu.roll(u, 1, 0) + cw[0:1] * pltpu.roll(u, 2, 0)
        return y[FFN_HALO:] + cb_ref[...]

    rows = (h_ref.shape[0] - FFN_HALO) // FFN_ROW_CHUNKS
    cols = pl.ds(pl.multiple_of(j * tf, tf), tf)
    for c in range(FFN_ROW_CHUNKS):
        h = h_ref[c * rows:(c + 1) * rows + FFN_HALO, :]
        a = conv_branch(h, wua_ref, cwa_ref, cba_ref)
        b = conv_branch(h, wub_ref, cwb_ref, cbb_ref)
        act_ref[c * rows:(c + 1) * rows, cols] = (_gelu_tanh(a) * b).astype(BF16)

    @pl.when(j == pl.num_programs(2) - 1)
    def _():
        y_ref[0] = x_ref[0] + _rmsnorm(_dot(act_ref[...], wd_ref[...]), gpost_ref[...])


def _ffn(x, gpre, w_up, conv_w, conv_b, w_down, gpost):
    n, T, D = x.shape
    tm, tf = TM_FFN, TF_FFN
    nff = D_FF // tf
    halo_blocks = tm // FFN_HALO
    return pl.pallas_call(
        _ffn_kernel,
        grid=(n, T // tm, nff),
        in_specs=[
            pl.BlockSpec((1, tm, D), lambda b, i, j: (b, i, 0)),
            pl.BlockSpec((1, FFN_HALO, D), lambda b, i, j: (b, jnp.maximum(i * halo_blocks - 1, 0), 0)),
            pl.BlockSpec((1, D), lambda b, i, j: (0, 0)),
            pl.BlockSpec((D, tf), lambda b, i, j: (0, j)),
            pl.BlockSpec((D, tf), lambda b, i, j: (0, nff + j)),
            pl.BlockSpec((CONV_W, tf), lambda b, i, j: (0, j)),
            pl.BlockSpec((CONV_W, tf), lambda b, i, j: (0, nff + j)),
            pl.BlockSpec((1, tf), lambda b, i, j: (0, j)),
            pl.BlockSpec((1, tf), lambda b, i, j: (0, nff + j)),
            pl.BlockSpec((D_FF, D), lambda b, i, j: (0, 0)),
            pl.BlockSpec((1, D), lambda b, i, j: (0, 0)),
        ],
        out_specs=pl.BlockSpec((1, tm, D), lambda b, i, j: (b, i, 0)),
        out_shape=jax.ShapeDtypeStruct((n, T, D), F32),
        scratch_shapes=[pltpu.VMEM((FFN_HALO + tm, D), BF16), pltpu.VMEM((tm, D_FF), BF16)],
        compiler_params=_params("parallel", "parallel", "arbitrary"),
        name="ffn",
    )(x, x, gpre, w_up, w_up, conv_w, conv_w, conv_b, conv_b, w_down, gpost)


def _block_overlap_t(nc, ns):
    cs = np.arange(nc)[None, :] * CMP_STRIDE
    ss = np.arange(ns)[:, None] * SLC_LEN
    ov = np.minimum(cs + CMP_LEN, ss + SLC_LEN) - np.maximum(cs, ss)
    return (np.clip(ov, 0, None) / CMP_STRIDE).astype(np.float32)


def kernel(x, g_mix_pre, g_mix_post, g_ffn_pre, g_ffn_post, w_in, cmp_pos, cmp_k_w1, cmp_k_b1, cmp_k_w2,
           cmp_v_w1, cmp_v_b1, cmp_v_w2, sinks, w_out, w_up, conv_w, conv_b, w_down):
    n, T, _ = x.shape
    depth = w_in.shape[0]
    nch = T // CMP_STRIDE
    assert T % (BLOCK_Q * B_PAIRS[-1][1]) == 0 and T % TM_FFN == 0 and nch % BLOCK_Q == 0

    def cols(i):
        return w_in[:, :, IN_OFFS[i]:IN_OFFS[i + 1]]

    (aq, akc, avc, aks, avs, akw, avw, agate, bq, bk, bv, cq, ck, cv) = [cols(i) for i in range(len(IN_SIZES))]
    order = jnp.asarray(C_HEAD_ORDER)
    cq = cq.reshape(depth, D_MODEL, C_HEADS, HEAD_DIM)[:, :, order].reshape(depth, D_MODEL, C_HEADS * HEAD_DIM)
    c_row0 = (A_HEADS + B_HEADS) * HEAD_DIM
    w_out = jnp.concatenate(
        [w_out[:, :c_row0], w_out[:, c_row0:].reshape(depth, C_HEADS, HEAD_DIM, D_MODEL)[:, order].reshape(depth, -1, D_MODEL)],
        axis=1)
    pair_cols = [a[:, :, p * B_W:(p + 1) * B_W] for p in range(len(B_PAIRS)) for a in (bq, bk, bv)]
    wz = jnp.concatenate([aq, cq, akc, avc, aks, akw, ck, cv] + pair_cols, axis=-1).astype(BF16)
    wtv = jnp.concatenate([avs, avw], axis=-1).transpose(0, 2, 1).astype(BF16)
    wtg = jnp.pad(agate, ((0, 0), (0, 0), (0, GATE_ROWS - agate.shape[-1]))).transpose(0, 2, 1).astype(BF16)
    pos = jnp.broadcast_to(cmp_pos.reshape(depth, 1, CMP_LEN * HEAD_DIM), (depth, 8, CMP_LEN * HEAD_DIM)).astype(BF16)
    w1k, w1v = cmp_k_w1.astype(BF16), cmp_v_w1.astype(BF16)
    w2k = cmp_k_w2.astype(BF16)
    w2vt = cmp_v_w2.transpose(0, 2, 1).astype(BF16)
    w_out_b, w_up_b, w_down_b = w_out.astype(BF16), w_up.astype(BF16), w_down.astype(BF16)
    ovt = jnp.asarray(_block_overlap_t(nch, T // SLC_LEN), BF16)

    tables = _nsa_tables(T)
    row = lambda a: a.reshape(1, -1)
    for l in range(depth):
        z, cvp, b0, b1, b2, vt, gt = _in_proj(x, row(g_mix_pre[l]), wz[l], wtv[l], wtg[l])
        xkv = (z[:, :, Z_AKC:Z_AKC + 2 * A_KV * HEAD_DIM]
               .reshape(n, nch, CMP_STRIDE, 2 * A_KV, HEAD_DIM)
               .transpose(0, 3, 1, 2, 4)
               .reshape(n, 2 * A_KV, nch, CMP_STRIDE * HEAD_DIM))
        kc, vct = _compress(xkv, pos[l], w1k[l], row(cmp_k_b1[l]), w2k[l], w1v[l], row(cmp_v_b1[l]), w2vt[l])
        oat = _nsa(z, kc, vct, vt, gt, ovt, tables)
        oc = _swa_sink(z, cvp, sinks[l])
        pairs = [_dilated_pair(bp, p) for p, bp in enumerate((b0, b1, b2))]
        x = _out_proj(x, oat, [o for o, _ in pairs], [s for _, s in pairs], oc, w_out_b[l], row(g_mix_post[l]))
        x = _ffn(x, row(g_ffn_pre[l]), w_up_b[l], conv_w[l], row(conv_b[l]), w_down_b[l], row(g_ffn_post[l]))
    return x
```

```python
import functools

import numpy as np
import jax
import jax.numpy as jnp
from jax import lax
from jax.experimental import pallas as pl
from jax.experimental.pallas import tpu as pltpu

D_MODEL = 1024
HEAD_DIM = 64
BLOCK_Q = 128
A_KV, A_REP = 2, 3
A_HEADS = A_KV * A_REP
CMP_LEN, CMP_STRIDE, CMP_HIDDEN = 32, 16, 256
SLC_LEN, SLC_TOPN = 64, 8
WIN_A = 512
B_PAIRS = ((128, 1), (512, 4), (2048, 16))
B_HEADS = 4
C_KV, C_REP = 2, 3
C_HEADS = C_KV * C_REP
WIN_C = 128
D_FF = 2816
CONV_W = 3
EPS = 1e-6

N_ALIBI = C_HEADS + len(B_PAIRS) * B_HEADS + A_HEADS
B_SLOPE_OFF = C_HEADS
A_SLOPE_OFF = C_HEADS + len(B_PAIRS) * B_HEADS
IN_SIZES = ((A_HEADS * HEAD_DIM,) + (A_KV * HEAD_DIM,) * 6 + (A_HEADS * 3,)
            + (len(B_PAIRS) * B_HEADS * HEAD_DIM,) * 3
            + (C_HEADS * HEAD_DIM, C_KV * HEAD_DIM, C_KV * HEAD_DIM))
IN_OFFS = tuple(int(v) for v in np.cumsum((0,) + IN_SIZES))

_SLOPES = tuple(float(v) for v in
                np.exp2(-8.0 * np.arange(1, N_ALIBI + 1, dtype=np.float32) / N_ALIBI).astype(np.float32))

Z_AQ, Z_CQ = 0, 384
Z_AKC, Z_AVC, Z_AKS, Z_AKW, Z_CK = 768, 896, 1024, 1152, 1280
ZA_W = 1408
W_CV = 1408
W_B = 1536
B_W = B_HEADS * HEAD_DIM
PAIR_W = 3 * B_W
PAIR_OUT_W = 2 * B_W + 2 * B_W
W_ALL = W_B + len(B_PAIRS) * PAIR_W
GATE_ROWS = 32
C_SLAB_HEADS = tuple((r, C_REP + r) for r in range(C_REP))
C_HEAD_ORDER = tuple(h for pair in C_SLAB_HEADS for h in pair)

F32 = jnp.float32
BF16 = jnp.bfloat16
NEG_INF = float("-inf")

TM_PROJ = 512
TM_FFN = 1024
TF_FFN = 256
FFN_HALO = 16
FFN_DOWN_CHUNKS = 3
BAND_QB = 4
SLC_CHUNK = 512
VT_ROWS = 80
VMEM_LIMIT = 56 * 1024 * 1024


def _dot(a, b):
    return jnp.dot(a, b, preferred_element_type=F32)


def _dot_nt(a, b):
    return lax.dot_general(a, b, (((1,), (1,)), ((), ())), preferred_element_type=F32)


def _rmsnorm(x, g):
    return x * lax.rsqrt(jnp.mean(x * x, axis=-1, keepdims=True) + EPS) * g


def _gelu_tanh(x):
    return 0.5 * x * (1.0 + jnp.tanh(0.7978845608028654 * (x + 0.044715 * (x * x * x))))


def _params(*sem):
    return pltpu.CompilerParams(dimension_semantics=sem, vmem_limit_bytes=VMEM_LIMIT)


def _pad_values(v):
    upper = lax.broadcasted_iota(jnp.int32, (v.shape[0], 128), 1) >= HEAD_DIM
    parts = []
    for s in range(v.shape[1] // 128):
        slab = v[:, s * 128:(s + 1) * 128].astype(BF16)
        parts += [jnp.where(upper, jnp.zeros_like(slab), slab), jnp.where(upper, slab, jnp.zeros_like(slab))]
    return jnp.concatenate(parts, axis=1)


def _in_proj_kernel(x_ref, g_ref, w_ref, wtv_ref, wtg_ref, za_ref, cv_ref, b0_ref, b1_ref, b2_ref,
                    vt_ref, gt_ref, slab_ref):
    h = _rmsnorm(x_ref[0], g_ref[...]).astype(BF16)
    tm = h.shape[0]
    half = ZA_W // 2
    for c in range(0, ZA_W, half):
        za_ref[0, :, c:c + half] = _dot(h, w_ref[:, c:c + half]).astype(BF16)
    cv_ref[0, 0] = _pad_values(_dot(h, w_ref[:, W_CV:W_B]))
    n_slab = PAIR_W // 128
    for p, b_ref in enumerate((b0_ref, b1_ref, b2_ref)):
        r = B_PAIRS[p][1]
        res = _dot(h, w_ref[:, W_B + p * PAIR_W:W_B + (p + 1) * PAIR_W])
        if r == 1:
            b_ref[0, 0] = jnp.concatenate([res[:, 0:2 * B_W].astype(BF16), _pad_values(res[:, 2 * B_W:])], axis=1)
            continue
        for c in range(n_slab):
            slab_ref[c] = res[:, c * 128:(c + 1) * 128]
        for j in range(r):
            rows = jnp.concatenate([slab_ref[c, pl.ds(j, tm // r, stride=r), :] for c in range(n_slab)], axis=1)
            b_ref[0, j] = jnp.concatenate([rows[:, 0:2 * B_W].astype(BF16), _pad_values(rows[:, 2 * B_W:])], axis=1)
    vt = _dot_nt(wtv_ref[...], h).astype(BF16)
    fill = VT_ROWS - HEAD_DIM
    ones_row = (lax.broadcasted_iota(jnp.int32, (fill, vt.shape[1]), 0) == 0).astype(BF16)
    for i in range(2 * A_KV):
        vt_ref[0, i * VT_ROWS:i * VT_ROWS + HEAD_DIM, :] = vt[i * HEAD_DIM:(i + 1) * HEAD_DIM]
        vt_ref[0, i * VT_ROWS + HEAD_DIM:(i + 1) * VT_ROWS, :] = ones_row
    gt_ref[0] = _dot_nt(wtg_ref[...], h)


def _in_proj(x, g, wz, wtv, wtg):
    n, T, D = x.shape
    tm = TM_PROJ
    return pl.pallas_call(
        _in_proj_kernel,
        grid=(n, T // tm),
        in_specs=[
            pl.BlockSpec((1, tm, D), lambda b, i: (b, i, 0)),
            pl.BlockSpec((1, D), lambda b, i: (0, 0)),
            pl.BlockSpec((D, W_ALL), lambda b, i: (0, 0)),
            pl.BlockSpec((4 * HEAD_DIM, D), lambda b, i: (0, 0)),
            pl.BlockSpec((GATE_ROWS, D), lambda b, i: (0, 0)),
        ],
        out_specs=[
            pl.BlockSpec((1, tm, ZA_W), lambda b, i: (b, i, 0)),
            pl.BlockSpec((1, 1, tm, 2 * C_KV * HEAD_DIM), lambda b, i: (b, 0, i, 0)),
        ] + [
            pl.BlockSpec((1, r, tm // r, PAIR_OUT_W), lambda b, i: (b, 0, i, 0)) for _, r in B_PAIRS
        ] + [
            pl.BlockSpec((1, 2 * A_KV * VT_ROWS, tm), lambda b, i: (b, 0, i)),
            pl.BlockSpec((1, GATE_ROWS, tm), lambda b, i: (b, 0, i)),
        ],
        out_shape=[
            jax.ShapeDtypeStruct((n, T, ZA_W), BF16),
            jax.ShapeDtypeStruct((n, 1, T, 2 * C_KV * HEAD_DIM), BF16),
        ] + [
            jax.ShapeDtypeStruct((n, r, T // r, PAIR_OUT_W), BF16) for _, r in B_PAIRS
        ] + [
            jax.ShapeDtypeStruct((n, 2 * A_KV * VT_ROWS, T), BF16),
            jax.ShapeDtypeStruct((n, GATE_ROWS, T), F32),
        ],
        scratch_shapes=[pltpu.VMEM((PAIR_W // 128, tm, 128), F32)],
        compiler_params=_params("parallel", "parallel"),
        name="in_proj",
    )(x, g, wz, wtv, wtg)


def _compress_kernel(xk_ref, xv_ref, pos_ref, w1k_ref, b1k_ref, w2k_ref,
                     w1v_ref, b1v_ref, w2vt_ref, kc_ref, vct_ref):
    half = CMP_STRIDE * HEAD_DIM

    def hidden(x, w1_ref, b1_ref):
        y1 = _dot(x, w1_ref[0:half, :])
        y2 = _dot(x, w1_ref[half:2 * half, :])
        bias = _dot(pos_ref[...], w1_ref[...])[0:1] + b1_ref[...]
        nrow = y2.shape[0]
        return _gelu_tanh(y1 + pltpu.roll(y2, nrow - 1, 0) + bias).astype(BF16)

    kc_ref[0, 0] = _dot(hidden(xk_ref[0, 0], w1k_ref, b1k_ref), w2k_ref[...]).astype(BF16)
    vct_ref[0, 0] = _dot_nt(w2vt_ref[...], hidden(xv_ref[0, 0], w1v_ref, b1v_ref)).astype(BF16)


def _compress(xkv, pos, w1k, b1k, w2k, w1v, b1v, w2vt):
    n, _, nch, kdim = xkv.shape
    full = lambda shape: pl.BlockSpec(shape, lambda b, g: (0,) * len(shape))
    return pl.pallas_call(
        _compress_kernel,
        grid=(n, A_KV),
        in_specs=[
            pl.BlockSpec((1, 1, nch, kdim), lambda b, g: (b, g, 0, 0)),
            pl.BlockSpec((1, 1, nch, kdim), lambda b, g: (b, A_KV + g, 0, 0)),
            full(pos.shape), full(w1k.shape), full(b1k.shape), full(w2k.shape),
            full(w1v.shape), full(b1v.shape), full(w2vt.shape),
        ],
        out_specs=[
            pl.BlockSpec((1, 1, nch, HEAD_DIM), lambda b, g: (b, g, 0, 0)),
            pl.BlockSpec((1, 1, HEAD_DIM, nch), lambda b, g: (b, g, 0, 0)),
        ],
        out_shape=[
            jax.ShapeDtypeStruct((n, A_KV, nch, HEAD_DIM), BF16),
            jax.ShapeDtypeStruct((n, A_KV, HEAD_DIM, nch), BF16),
        ],
        compiler_params=_params("parallel", "parallel"),
        name="compress",
    )(xkv, xkv, pos, w1k, b1k, w2k, w1v, b1v, w2vt)


def _nsa_tables(T):
    sl = np.asarray(_SLOPES[A_SLOPE_OFF:A_SLOPE_OFF + A_HEADS], np.float32).reshape(A_KV, A_REP)
    slope = np.repeat(sl, BLOCK_Q, axis=1)[:, None, :]
    tl = np.tile(np.arange(BLOCK_Q, dtype=np.float32), A_REP)[None, None, :]
    rows = lambda n: np.arange(n, dtype=np.float32)[None, :, None]
    neg = np.float32(NEG_INF)
    p0 = slope * rows(SLC_LEN)
    pj = slope * (rows(T // SLC_LEN) * SLC_LEN)
    pc = slope * (rows(T // CMP_STRIDE) * CMP_STRIDE + np.float32((CMP_LEN - 1) / 2))
    kl = rows(BLOCK_Q)
    pd = np.where(kl <= tl, slope * (kl % SLC_LEN), neg)
    npb = -(-(WIN_A - 1) // BLOCK_Q)
    d = npb * BLOCK_Q + tl - rows((2 * npb + 1) * BLOCK_Q)
    bwin = np.where((d >= 0) & (d <= WIN_A - 1), -slope * d, neg)
    return [jnp.asarray(a.astype(np.float32)) for a in (p0, pj, pc, pd, bwin)]


def _nsa_kernel(q_ref, kc_ref, vct_ref, ks_ref, kw_ref, vt_ref, gt_ref, ovt_ref,
                p0_ref, pj_ref, pc_ref, pd_ref, bwin_ref, o_ref,
                seln_ref, selp_ref, s_ref, p_ref, acc_ref, m_ref):
    qb = pl.program_id(1)
    t0 = pl.multiple_of(qb * BLOCK_Q, BLOCK_Q)
    nl = A_REP * BLOCK_Q
    T = ks_ref.shape[1]
    ncmp = kc_ref.shape[2]
    nsel = ovt_ref.shape[0]
    lane = lax.broadcasted_iota(jnp.int32, (1, nl), 1)
    t_row = t0 + (lane & (BLOCK_Q - 1))
    sig = jax.nn.sigmoid(gt_ref[0])
    q = q_ref[0]
    blocks_per_q = BLOCK_Q // SLC_LEN

    groups = []
    for g in range(A_KV):
        heads = [g * A_REP + r for r in range(A_REP)]
        q3 = jnp.concatenate([q[:, h * HEAD_DIM:(h + 1) * HEAD_DIM] for h in heads], axis=0) * 0.125
        kcols = slice(g * HEAD_DIM, (g + 1) * HEAD_DIM)
        vs_rows = slice(g * VT_ROWS, (g + 1) * VT_ROWS)
        vw_rows = slice((A_KV + g) * VT_ROWS, (A_KV + g + 1) * VT_ROWS)

        st = _dot_nt(kc_ref[0, g], q3) + pc_ref[g]
        cst = lax.broadcasted_iota(jnp.int32, (ncmp, nl), 0) * CMP_STRIDE
        st = jnp.where((cst + (CMP_LEN - 1)) <= t_row, st, NEG_INF)
        m = jnp.max(st, axis=0, keepdims=True)
        m = jnp.where(m == NEG_INF, 0.0, m)
        p = jnp.exp(st - m)
        den = jnp.sum(p, axis=0, keepdims=True)
        pc = p / jnp.where(den > 0, den, 1.0)
        o_cmp = _dot(vct_ref[0, g], pc.astype(BF16))
        psum = pc[:, 0:BLOCK_Q] + pc[:, BLOCK_Q:2 * BLOCK_Q] + pc[:, 2 * BLOCK_Q:3 * BLOCK_Q]
        imp = _dot(ovt_ref[...], psum.astype(BF16))

        j = lax.broadcasted_iota(jnp.int32, (nsel, BLOCK_Q), 0)
        t128 = t0 + lax.broadcasted_iota(jnp.int32, (nsel, BLOCK_Q), 1)
        cur = t128 // SLC_LEN
        forced = (j == 0) | (j == cur) | (j == cur - 1)
        started = j * SLC_LEN <= t128
        score = jnp.where(started, jnp.where(forced, float("inf"), imp), NEG_INF)
        ahead = jnp.zeros((nsel, BLOCK_Q), jnp.int32)
        for k in range(nsel):
            rk = score[k:k + 1, :]
            ahead = ahead + ((rk > score) | ((rk == score) & (j > k))).astype(jnp.int32)
        keep = jnp.where((ahead < SLC_TOPN) & started, 0.0, NEG_INF)
        seln = jnp.concatenate([keep] * A_REP, axis=1) + pj_ref[g]
        seln_ref[g] = seln
        jrow = lax.broadcasted_iota(jnp.int32, (nsel, nl), 0)
        selp_ref[g] = jnp.where(jrow < blocks_per_q * qb, seln, NEG_INF)

        sd = _dot_nt(ks_ref[0, pl.ds(t0, BLOCK_Q), kcols], q3) + pd_ref[g]
        sd = sd + jnp.concatenate(
            [jnp.broadcast_to(seln_ref[g, pl.ds(blocks_per_q * qb + b, 1), :], (SLC_LEN, nl))
             for b in range(blocks_per_q)], axis=0)
        m_d = jnp.max(sd, axis=0, keepdims=True)
        m_ref[g] = m_d
        acc_ref[g] = jnp.zeros(acc_ref.shape[1:], F32)

        npb = -(-(WIN_A - 1) // BLOCK_Q)
        kwid = (npb + 1) * BLOCK_Q
        kst = pl.multiple_of(jnp.maximum(qb - npb, 0) * BLOCK_Q, BLOCK_Q)
        off = pl.multiple_of(kst + (npb - qb) * BLOCK_Q, BLOCK_Q)
        s = _dot_nt(kw_ref[0, pl.ds(kst, kwid), kcols], q3) + bwin_ref[g, pl.ds(off, kwid), :]
        m = jnp.max(s, axis=0, keepdims=True)
        acc = _dot(vt_ref[0, vw_rows, pl.ds(kst, kwid)], jnp.exp(s - m).astype(BF16))
        o_win = acc[0:HEAD_DIM] / acc[HEAD_DIM:HEAD_DIM + 1]
        groups.append((heads, q3, kcols, vs_rows, sd, m_d, o_cmp, o_win))

    def past(kv):
        for g, (heads, q3, kcols, vs_rows, sd, m_d, o_cmp, o_win) in enumerate(groups):
            p0 = p0_ref[g]
            mx = jnp.full((SLC_LEN, nl), NEG_INF, F32)
            for c0 in range(0, kv, SLC_CHUNK):
                s = _dot_nt(ks_ref[0, c0:c0 + SLC_CHUNK, kcols], q3)
                for b in range(SLC_CHUNK // SLC_LEN):
                    jb = c0 // SLC_LEN + b
                    sb = s[b * SLC_LEN:(b + 1) * SLC_LEN] + p0 + selp_ref[g, jb:jb + 1, :]
                    s_ref[g, jb * SLC_LEN:(jb + 1) * SLC_LEN, :] = sb
                    mx = jnp.maximum(mx, sb)
            m_all = jnp.maximum(m_d, jnp.max(mx, axis=0, keepdims=True))
            for c0 in range(0, kv, SLC_CHUNK):
                p_ref[g, c0:c0 + SLC_CHUNK, :] = jnp.exp(s_ref[g, c0:c0 + SLC_CHUNK, :] - m_all).astype(BF16)
            acc_ref[g] = _dot(vt_ref[0, vs_rows, 0:kv], p_ref[g, 0:kv, :])
            m_ref[g] = m_all

    q_per_chunk = SLC_CHUNK // BLOCK_Q
    for v in range(1, T // SLC_CHUNK + 1):
        pl.when((qb > q_per_chunk * (v - 1)) & (qb <= q_per_chunk * v))(functools.partial(past, v * SLC_CHUNK))

    for g, (heads, q3, kcols, vs_rows, sd, m_d, o_cmp, o_win) in enumerate(groups):
        p_d = jnp.exp(sd - m_ref[g]).astype(BF16)
        acc = acc_ref[g] + _dot(vt_ref[0, vs_rows, pl.ds(t0, BLOCK_Q)], p_d)
        o_slc = acc[0:HEAD_DIM] / acc[HEAD_DIM:HEAD_DIM + 1]

        for r, h in enumerate(heads):
            cols = slice(r * BLOCK_Q, (r + 1) * BLOCK_Q)
            o = (sig[3 * h:3 * h + 1] * o_cmp[:, cols] + sig[3 * h + 1:3 * h + 2] * o_slc[:, cols]
                 + sig[3 * h + 2:3 * h + 3] * o_win[:, cols])
            o_ref[0, h * HEAD_DIM:(h + 1) * HEAD_DIM, :] = o


def _nsa(z, kc, vct, vt, gt, ovt, tables):
    n, T, _ = z.shape
    ncmp = kc.shape[2]
    nsel = ovt.shape[0]
    kv_w = A_KV * HEAD_DIM
    nl = A_REP * BLOCK_Q
    const = lambda a: pl.BlockSpec(a.shape, lambda b, i: (0,) * a.ndim)
    return pl.pallas_call(
        _nsa_kernel,
        grid=(n, T // BLOCK_Q),
        in_specs=[
            pl.BlockSpec((1, BLOCK_Q, A_HEADS * HEAD_DIM), lambda b, i: (b, i, Z_AQ // (A_HEADS * HEAD_DIM))),
            pl.BlockSpec((1, A_KV, ncmp, HEAD_DIM), lambda b, i: (b, 0, 0, 0)),
            pl.BlockSpec((1, A_KV, HEAD_DIM, ncmp), lambda b, i: (b, 0, 0, 0)),
            pl.BlockSpec((1, T, kv_w), lambda b, i: (b, 0, Z_AKS // kv_w)),
            pl.BlockSpec((1, T, kv_w), lambda b, i: (b, 0, Z_AKW // kv_w)),
            pl.BlockSpec((1, 2 * A_KV * VT_ROWS, T), lambda b, i: (b, 0, 0)),
            pl.BlockSpec((1, GATE_ROWS, BLOCK_Q), lambda b, i: (b, 0, i)),
            const(ovt),
        ] + [const(a) for a in tables],
        out_specs=pl.BlockSpec((1, A_HEADS * HEAD_DIM, BLOCK_Q), lambda b, i: (b, 0, i)),
        out_shape=jax.ShapeDtypeStruct((n, A_HEADS * HEAD_DIM, T), F32),
        scratch_shapes=[
            pltpu.VMEM((A_KV, nsel, nl), F32), pltpu.VMEM((A_KV, nsel, nl), F32),
            pltpu.VMEM((A_KV, T, nl), F32), pltpu.VMEM((A_KV, T, nl), BF16),
            pltpu.VMEM((A_KV, VT_ROWS, nl), F32), pltpu.VMEM((A_KV, 1, nl), F32),
        ],
        compiler_params=_params("parallel", "parallel"),
        name="nsa",
    )(z, kc, vct, z, z, vt, gt, ovt, *tables)


def _band_table(slopes, dist_scale, past, kwid):
    npb = -(-past // BLOCK_Q)
    i = np.arange(BLOCK_Q)[:, None]
    j = np.arange(kwid + npb * BLOCK_Q)[None, :]
    diff = npb * BLOCK_Q + i - j
    sl = np.asarray(slopes, np.float32)[:, None, None]
    tab = np.where((diff >= 0) & (diff <= past), -sl * (diff * dist_scale).astype(np.float32), np.float32(NEG_INF))
    return jnp.asarray(tab.astype(np.float32))


def _banded_kernel(*refs, q_slabs, kv_slabs, head_of, past, kwid, n_sub, n_qb, has_sinks, want_lse):
    refs = list(refs)
    sink_ref = refs.pop(0) if has_sinks else None
    tab_ref, q_ref, k_ref, v_ref, o_ref = refs[:5]
    lse_ref = refs[5] if want_lse else None
    npb = -(-past // BLOCK_Q)
    cr = min(BLOCK_Q, 16 * 1024 // kwid)
    upper = lax.broadcasted_iota(jnp.int32, (cr, 128), 1) >= HEAD_DIM
    ones = jnp.ones((kwid, 128), BF16)
    for sub in range(n_sub):
        for part in range(n_qb * BLOCK_Q // cr):
            qb = pl.program_id(2) * n_qb + part * cr // BLOCK_Q
            kst = pl.multiple_of(jnp.maximum(qb - npb, 0) * BLOCK_Q, BLOCK_Q)
            off = pl.multiple_of(kst + (npb - qb) * BLOCK_Q, BLOCK_Q)
            rows = slice(part * cr, (part + 1) * cr)
            trow = slice(part * cr % BLOCK_Q, part * cr % BLOCK_Q + cr)
            for s_out in range(q_slabs):
                lanes = slice(s_out * 128, (s_out + 1) * 128)
                q = q_ref[0, sub, rows, lanes]
                kv = s_out * kv_slabs // q_slabs
                kwin = k_ref[0, sub, pl.ds(kst, kwid), kv * 128:(kv + 1) * 128]
                out, lses = [], []
                for half in range(2):
                    h = head_of[s_out][half]
                    qm = jnp.where(upper == (half == 1), q, jnp.zeros_like(q)) * 0.125
                    sr = _dot_nt(qm, kwin) + tab_ref[h, trow, pl.ds(off, kwid)]
                    m = jnp.max(sr, axis=1, keepdims=True)
                    if has_sinks:
                        m = jnp.maximum(m, sink_ref[h])
                    p = jnp.exp(sr - m).astype(BF16)
                    vpad = v_ref[0, sub, pl.ds(kst, kwid), (2 * kv + half) * 128:(2 * kv + half + 1) * 128]
                    den = _dot(p, ones)
                    if has_sinks:
                        den = den + jnp.exp(sink_ref[h] - m)
                    out.append(_dot(p, vpad) / den)
                    lses.append(m + jnp.log(den))
                o_ref[0, sub, rows, lanes] = (out[0] + out[1]).astype(o_ref.dtype)
                if want_lse:
                    lse_ref[0, sub, rows, lanes] = jnp.where(upper, lses[1], lses[0])


def _banded(q_arr, k_arr, v_arr, q_col, k_col, v_col, *, head_of, kv_heads, past, slopes, dist_scale,
            n_sub, n_qb, sinks, want_lse, out_dtype, name):
    n, n_seq, tf, _ = q_arr.shape
    qw, kw, vw = 128 * len(head_of), kv_heads * HEAD_DIM, 2 * kv_heads * HEAD_DIM
    kwid = min((-(-past // BLOCK_Q) + 1) * BLOCK_Q, tf)
    tab = _band_table(slopes, dist_scale, past, kwid)
    kern = functools.partial(_banded_kernel, q_slabs=len(head_of), kv_slabs=kw // 128, head_of=head_of,
                             past=past, kwid=kwid, n_sub=n_sub, n_qb=n_qb, has_sinks=sinks is not None,
                             want_lse=want_lse)
    rows = n_qb * BLOCK_Q
    out_spec = pl.BlockSpec((1, n_sub, rows, qw), lambda b, j, i: (b, j, i, 0))
    out_shape = jax.ShapeDtypeStruct((n, n_seq, tf, qw), out_dtype)
    in_specs = [
        pl.BlockSpec(tab.shape, lambda b, j, i: (0, 0, 0)),
        pl.BlockSpec((1, n_sub, rows, qw), lambda b, j, i: (b, j, i, q_col)),
        pl.BlockSpec((1, n_sub, tf, kw), lambda b, j, i: (b, j, 0, k_col)),
        pl.BlockSpec((1, n_sub, tf, vw), lambda b, j, i: (b, j, 0, v_col)),
    ]
    args = [tab, q_arr, k_arr, v_arr]
    if sinks is not None:
        in_specs.insert(0, pl.BlockSpec(memory_space=pltpu.SMEM))
        args.insert(0, sinks)
    return pl.pallas_call(
        kern,
        grid=(n, n_seq // n_sub, tf // rows),
        in_specs=in_specs,
        out_specs=[out_spec, out_spec] if want_lse else out_spec,
        out_shape=[out_shape, jax.ShapeDtypeStruct(out_shape.shape, F32)] if want_lse else out_shape,
        compiler_params=_params("parallel", "parallel", "parallel"),
        name=name,
    )(*args)


def _swa_sink(za, cv, sinks):
    n, T, _ = za.shape
    za4 = za.reshape(n, 1, T, ZA_W)
    return _banded(za4, za4, cv, Z_CQ // (C_HEADS * HEAD_DIM), Z_CK // (C_KV * HEAD_DIM), 0,
                   head_of=C_SLAB_HEADS, kv_heads=C_KV, past=WIN_C - 1, slopes=_SLOPES[:C_HEADS], dist_scale=1,
                   n_sub=1, n_qb=BAND_QB, sinks=sinks, want_lse=False, out_dtype=BF16, name="swa_sink")


def _dilated_pair(bp, pair):
    w, r = B_PAIRS[pair]
    tf = bp.shape[2]
    n_qb = min(BAND_QB, tf // BLOCK_Q)
    n_sub = min(r, BAND_QB // n_qb)
    hs = B_SLOPE_OFF + pair * B_HEADS
    head_of = tuple((2 * s, 2 * s + 1) for s in range(B_HEADS // 2))
    return _banded(bp, bp, bp, 0, 1, 1, head_of=head_of, kv_heads=B_HEADS, past=w // r,
                   slopes=_SLOPES[hs:hs + B_HEADS], dist_scale=r, n_sub=n_sub, n_qb=n_qb, sinks=None,
                   want_lse=True, out_dtype=F32, name=f"dilated_{pair}")


def _out_proj_kernel(x_ref, oat_ref, ob0_ref, ob1_ref, ob2_ref, l0_ref, l1_ref, l2_ref,
                     oc_ref, w_ref, g_ref, y_ref, slab_ref):
    tm = x_ref.shape[1]
    n_slab = B_W // 128

    def unfold(ref, slot):
        r = ref.shape[1]
        if r == 1:
            return ref[0, 0]
        for j in range(r):
            for c in range(n_slab):
                slab_ref[slot * n_slab + c, pl.ds(j, tm // r, stride=r), :] = ref[0, j, :, c * 128:(c + 1) * 128]
        return jnp.concatenate([slab_ref[slot * n_slab + c] for c in range(n_slab)], axis=1)

    obs = [unfold(ref, i) for i, ref in enumerate((ob0_ref, ob1_ref, ob2_ref))]
    l0, l1, l2 = [unfold(ref, 3 + i) for i, ref in enumerate((l0_ref, l1_ref, l2_ref))]
    mx = jnp.maximum(jnp.maximum(l0, l1), l2)
    e0, e1, e2 = jnp.exp(l0 - mx), jnp.exp(l1 - mx), jnp.exp(l2 - mx)
    ob = (e0 * obs[0] + e1 * obs[1] + e2 * obs[2]) / (e0 + e1 + e2)
    wa = A_HEADS * HEAD_DIM
    wb = wa + B_HEADS * HEAD_DIM
    oa = oat_ref[0].T.astype(BF16)
    mix = (_dot(oa, w_ref[0:wa, :]) + _dot(ob.astype(BF16), w_ref[wa:wb, :])
           + _dot(oc_ref[0, 0], w_ref[wb:, :]))
    y_ref[0] = x_ref[0] + _rmsnorm(mix, g_ref[...])


def _out_proj(x, oat, obs, lses, oc, w, g):
    n, T, D = x.shape
    tm = TM_PROJ
    wa, wc = A_HEADS * HEAD_DIM, C_HEADS * HEAD_DIM
    row = lambda width: pl.BlockSpec((1, tm, width), lambda b, i: (b, i, 0))
    folded = lambda a: pl.BlockSpec((1, a.shape[1], tm // a.shape[1], a.shape[3]), lambda b, i: (b, 0, i, 0))
    return pl.pallas_call(
        _out_proj_kernel,
        grid=(n, T // tm),
        in_specs=[row(D), pl.BlockSpec((1, wa, tm), lambda b, i: (b, 0, i))]
        + [folded(a) for a in obs] + [folded(a) for a in lses]
        + [folded(oc), pl.BlockSpec((D, D), lambda b, i: (0, 0)), pl.BlockSpec((1, D), lambda b, i: (0, 0))],
        out_specs=row(D),
        out_shape=jax.ShapeDtypeStruct((n, T, D), F32),
        scratch_shapes=[pltpu.VMEM((2 * len(B_PAIRS) * (B_W // 128), tm, 128), F32)],
        compiler_params=_params("parallel", "parallel"),
        name="out_proj",
    )(x, oat, *obs, *lses, oc, w, g)


def _ffn_kernel(x_ref, xp_ref, gpre_ref, wu_ref, cw_ref, cb_ref, wd_ref, gpost_ref, y_ref, h_ref, act_ref):
    ti = pl.program_id(1)
    tf = TF_FFN
    hp = _rmsnorm(xp_ref[0], gpre_ref[...])
    h_ref[0:FFN_HALO, :] = jnp.where(ti > 0, hp, 0.0).astype(BF16)
    h_ref[FFN_HALO:, :] = _rmsnorm(x_ref[0], gpre_ref[...]).astype(BF16)
    h = h_ref[...]

    def conv_branch(c0):
        u = _dot(h, wu_ref[:, c0:c0 + tf])
        cw = cw_ref[:, c0:c0 + tf]
        y = cw[2:3] * u + cw[1:2] * pltpu.roll(u, 1, 0) + cw[0:1] * pltpu.roll(u, 2, 0)
        return y[FFN_HALO:] + cb_ref[:, c0:c0 + tf]

    nff = D_FF // tf
    bounds = [nff * (i + 1) // FFN_DOWN_CHUNKS for i in range(FFN_DOWN_CHUNKS)]
    done = 0
    for k in range(nff):
        a = conv_branch(k * tf)
        b = conv_branch(D_FF + k * tf)
        act_ref[:, k * tf:(k + 1) * tf] = (_gelu_tanh(a) * b).astype(BF16)
        if k + 1 in bounds:
            lo, hi = done * tf, (k + 1) * tf
            part = _dot(act_ref[:, lo:hi], wd_ref[lo:hi, :])
            if k + 1 == nff:
                total = part if done == 0 else y_ref[0] + part
                y_ref[0] = x_ref[0] + _rmsnorm(total, gpost_ref[...])
            elif done == 0:
                y_ref[0] = part
            else:
                y_ref[0] += part
            done = k + 1


def _ffn(x, gpre, w_up, conv_w, conv_b, w_down, gpost):
    n, T, D = x.shape
    tm = TM_FFN
    halo_blocks = tm // FFN_HALO
    resident = lambda a: pl.BlockSpec(a.shape, lambda b, i: (0,) * a.ndim, pipeline_mode=pl.Buffered(1))
    return pl.pallas_call(
        _ffn_kernel,
        grid=(n, T // tm),
        in_specs=[
            pl.BlockSpec((1, tm, D), lambda b, i: (b, i, 0)),
            pl.BlockSpec((1, FFN_HALO, D), lambda b, i: (b, jnp.maximum(i * halo_blocks - 1, 0), 0)),
            resident(gpre), resident(w_up), resident(conv_w), resident(conv_b), resident(w_down), resident(gpost),
        ],
        out_specs=pl.BlockSpec((1, tm, D), lambda b, i: (b, i, 0)),
        out_shape=jax.ShapeDtypeStruct((n, T, D), F32),
        scratch_shapes=[pltpu.VMEM((FFN_HALO + tm, D), BF16), pltpu.VMEM((tm, D_FF), BF16)],
        compiler_params=_params("parallel", "parallel"),
        name="ffn",
    )(x, x, gpre, w_up, conv_w, conv_b, w_down, gpost)


def _block_overlap_t(nc, ns):
    cs = np.arange(nc)[None, :] * CMP_STRIDE
    ss = np.arange(ns)[:, None] * SLC_LEN
    ov = np.minimum(cs + CMP_LEN, ss + SLC_LEN) - np.maximum(cs, ss)
    return (np.clip(ov, 0, None) / CMP_STRIDE).astype(np.float32)


def kernel(x, g_mix_pre, g_mix_post, g_ffn_pre, g_ffn_post, w_in, cmp_pos, cmp_k_w1, cmp_k_b1, cmp_k_w2,
           cmp_v_w1, cmp_v_b1, cmp_v_w2, sinks, w_out, w_up, conv_w, conv_b, w_down):
    n, T, _ = x.shape
    depth = w_in.shape[0]
    nch = T // CMP_STRIDE
    assert T % (BLOCK_Q * B_PAIRS[-1][1]) == 0 and T % TM_FFN == 0 and nch % BLOCK_Q == 0

    def cols(i):
        return w_in[:, :, IN_OFFS[i]:IN_OFFS[i + 1]]

    (aq, akc, avc, aks, avs, akw, avw, agate, bq, bk, bv, cq, ck, cv) = [cols(i) for i in range(len(IN_SIZES))]
    order = jnp.asarray(C_HEAD_ORDER)
    cq = cq.reshape(depth, D_MODEL, C_HEADS, HEAD_DIM)[:, :, order].reshape(depth, D_MODEL, C_HEADS * HEAD_DIM)
    c_row0 = (A_HEADS + B_HEADS) * HEAD_DIM
    w_out = jnp.concatenate(
        [w_out[:, :c_row0], w_out[:, c_row0:].reshape(depth, C_HEADS, HEAD_DIM, D_MODEL)[:, order].reshape(depth, -1, D_MODEL)],
        axis=1)
    pair_cols = [a[:, :, p * B_W:(p + 1) * B_W] for p in range(len(B_PAIRS)) for a in (bq, bk, bv)]
    wz = jnp.concatenate([aq, cq, akc, avc, aks, akw, ck, cv] + pair_cols, axis=-1).astype(BF16)
    wtv = jnp.concatenate([avs, avw], axis=-1).transpose(0, 2, 1).astype(BF16)
    wtg = jnp.pad(agate, ((0, 0), (0, 0), (0, GATE_ROWS - agate.shape[-1]))).transpose(0, 2, 1).astype(BF16)
    pos = jnp.broadcast_to(cmp_pos.reshape(depth, 1, CMP_LEN * HEAD_DIM), (depth, 8, CMP_LEN * HEAD_DIM)).astype(BF16)
    w1k, w1v = cmp_k_w1.astype(BF16), cmp_v_w1.astype(BF16)
    w2k = cmp_k_w2.astype(BF16)
    w2vt = cmp_v_w2.transpose(0, 2, 1).astype(BF16)
    w_out_b, w_up_b, w_down_b = w_out.astype(BF16), w_up.astype(BF16), w_down.astype(BF16)
    ovt = jnp.asarray(_block_overlap_t(nch, T // SLC_LEN), BF16)

    tables = _nsa_tables(T)
    row = lambda a: a.reshape(1, -1)
    for l in range(depth):
        z, cvp, b0, b1, b2, vt, gt = _in_proj(x, row(g_mix_pre[l]), wz[l], wtv[l], wtg[l])
        xkv = (z[:, :, Z_AKC:Z_AKC + 2 * A_KV * HEAD_DIM]
               .reshape(n, nch, CMP_STRIDE, 2 * A_KV, HEAD_DIM)
               .transpose(0, 3, 1, 2, 4)
               .reshape(n, 2 * A_KV, nch, CMP_STRIDE * HEAD_DIM))
        kc, vct = _compress(xkv, pos[l], w1k[l], row(cmp_k_b1[l]), w2k[l], w1v[l], row(cmp_v_b1[l]), w2vt[l])
        oat = _nsa(z, kc, vct, vt, gt, ovt, tables)
        oc = _swa_sink(z, cvp, sinks[l])
        pairs = [_dilated_pair(bp, p) for p, bp in enumerate((b0, b1, b2))]
        x = _out_proj(x, oat, [o for o, _ in pairs], [s for _, s in pairs], oc, w_out_b[l], row(g_mix_post[l]))
        x = _ffn(x, row(g_ffn_pre[l]), w_up_b[l], conv_w[l], row(conv_b[l]), w_down_b[l], row(g_ffn_post[l]))
    return x
```

```python
import functools

import numpy as np
import jax
import jax.numpy as jnp
from jax import lax
from jax.experimental import pallas as pl
from jax.experimental.pallas import tpu as pltpu

D_MODEL = 1024
HEAD_DIM = 64
BLOCK_Q = 128
A_KV, A_REP = 2, 3
A_HEADS = A_KV * A_REP
CMP_LEN, CMP_STRIDE, CMP_HIDDEN = 32, 16, 256
SLC_LEN, SLC_TOPN = 64, 8
WIN_A = 512
B_PAIRS = ((128, 1), (512, 4), (2048, 16))
B_HEADS = 4
C_KV, C_REP = 2, 3
C_HEADS = C_KV * C_REP
WIN_C = 128
D_FF = 2816
CONV_W = 3
EPS = 1e-6

N_ALIBI = C_HEADS + len(B_PAIRS) * B_HEADS + A_HEADS
B_SLOPE_OFF = C_HEADS
A_SLOPE_OFF = C_HEADS + len(B_PAIRS) * B_HEADS
IN_SIZES = ((A_HEADS * HEAD_DIM,) + (A_KV * HEAD_DIM,) * 6 + (A_HEADS * 3,)
            + (len(B_PAIRS) * B_HEADS * HEAD_DIM,) * 3
            + (C_HEADS * HEAD_DIM, C_KV * HEAD_DIM, C_KV * HEAD_DIM))
IN_OFFS = tuple(int(v) for v in np.cumsum((0,) + IN_SIZES))

_SLOPES = tuple(float(v) for v in
                np.exp2(-8.0 * np.arange(1, N_ALIBI + 1, dtype=np.float32) / N_ALIBI).astype(np.float32))

Z_AQ, Z_CQ = 0, 384
Z_AKC, Z_AVC, Z_AKS, Z_AKW, Z_CK = 768, 896, 1024, 1152, 1280
ZA_W = 1408
W_CV = 1408
W_B = 1536
B_W = B_HEADS * HEAD_DIM
PAIR_W = 3 * B_W
PAIR_OUT_W = 2 * B_W + 2 * B_W
W_ALL = W_B + len(B_PAIRS) * PAIR_W
GATE_ROWS = 32
C_SLAB_HEADS = tuple((r, C_REP + r) for r in range(C_REP))
C_HEAD_ORDER = tuple(h for pair in C_SLAB_HEADS for h in pair)

F32 = jnp.float32
BF16 = jnp.bfloat16
NEG_INF = float("-inf")

TM_PROJ = 512
TM_IN = 1024
TM_FFN = 1024
TF_FFN = 256
FFN_HALO = 16
FFN_DOWN_CHUNKS = 3
BAND_QB = 4
SLC_CHUNK = 512
VT_ROWS = 80
VMEM_LIMIT = 56 * 1024 * 1024


def _dot(a, b):
    return jnp.dot(a, b, preferred_element_type=F32)


def _dot_nt(a, b):
    return lax.dot_general(a, b, (((1,), (1,)), ((), ())), preferred_element_type=F32)


def _rmsnorm(x, g):
    return x * lax.rsqrt(jnp.mean(x * x, axis=-1, keepdims=True) + EPS) * g


def _gelu_tanh(x):
    return 0.5 * x * (1.0 + jnp.tanh(0.7978845608028654 * (x + 0.044715 * (x * x * x))))


def _params(*sem):
    return pltpu.CompilerParams(dimension_semantics=sem, vmem_limit_bytes=VMEM_LIMIT)


def _layer(a, l, **kw):
    return pl.BlockSpec((None,) + a.shape[1:], lambda *_: (l,) + (0,) * (a.ndim - 1), **kw)


def _pad_values(v):
    upper = lax.broadcasted_iota(jnp.int32, (v.shape[0], 128), 1) >= HEAD_DIM
    parts = []
    for s in range(v.shape[1] // 128):
        slab = v[:, s * 128:(s + 1) * 128].astype(BF16)
        parts += [jnp.where(upper, jnp.zeros_like(slab), slab), jnp.where(upper, slab, jnp.zeros_like(slab))]
    return jnp.concatenate(parts, axis=1)


def _in_proj_kernel(x_ref, g_ref, w_ref, wtv_ref, wtg_ref, za_ref, cv_ref, b0_ref, b1_ref, b2_ref,
                    vt_ref, gt_ref, slab_ref):
    h = _rmsnorm(x_ref[0], g_ref[...]).astype(BF16)
    tm = h.shape[0]
    half = ZA_W // 2
    for c in range(0, ZA_W, half):
        za_ref[0, :, c:c + half] = _dot(h, w_ref[:, c:c + half]).astype(BF16)
    cv_ref[0, 0] = _pad_values(_dot(h, w_ref[:, W_CV:W_B]))
    n_slab = PAIR_W // 128
    for p, b_ref in enumerate((b0_ref, b1_ref, b2_ref)):
        r = B_PAIRS[p][1]
        res = _dot(h, w_ref[:, W_B + p * PAIR_W:W_B + (p + 1) * PAIR_W])
        if r == 1:
            b_ref[0, 0] = jnp.concatenate([res[:, 0:2 * B_W].astype(BF16), _pad_values(res[:, 2 * B_W:])], axis=1)
            continue
        for c in range(n_slab):
            slab_ref[c] = res[:, c * 128:(c + 1) * 128]
        for j in range(r):
            rows = jnp.concatenate([slab_ref[c, pl.ds(j, tm // r, stride=r), :] for c in range(n_slab)], axis=1)
            b_ref[0, j] = jnp.concatenate([rows[:, 0:2 * B_W].astype(BF16), _pad_values(rows[:, 2 * B_W:])], axis=1)
    vt = _dot_nt(wtv_ref[...], h).astype(BF16)
    fill = VT_ROWS - HEAD_DIM
    ones_row = (lax.broadcasted_iota(jnp.int32, (fill, vt.shape[1]), 0) == 0).astype(BF16)
    for i in range(2 * A_KV):
        vt_ref[0, i * VT_ROWS:i * VT_ROWS + HEAD_DIM, :] = vt[i * HEAD_DIM:(i + 1) * HEAD_DIM]
        vt_ref[0, i * VT_ROWS + HEAD_DIM:(i + 1) * VT_ROWS, :] = ones_row
    gt_ref[0] = _dot_nt(wtg_ref[...], h)


def _in_proj(x, g, wz, wtv, wtg, l):
    n, T, D = x.shape
    tm = TM_IN
    return pl.pallas_call(
        _in_proj_kernel,
        grid=(n, T // tm),
        in_specs=[
            pl.BlockSpec((1, tm, D), lambda b, i: (b, i, 0)),
            pl.BlockSpec((1, D), lambda b, i: (0, 0)),
            _layer(wz, l, pipeline_mode=pl.Buffered(1)), _layer(wtv, l), _layer(wtg, l),
        ],
        out_specs=[
            pl.BlockSpec((1, tm, ZA_W), lambda b, i: (b, i, 0)),
            pl.BlockSpec((1, 1, tm, 2 * C_KV * HEAD_DIM), lambda b, i: (b, 0, i, 0)),
        ] + [
            pl.BlockSpec((1, r, tm // r, PAIR_OUT_W), lambda b, i: (b, 0, i, 0)) for _, r in B_PAIRS
        ] + [
            pl.BlockSpec((1, 2 * A_KV * VT_ROWS, tm), lambda b, i: (b, 0, i)),
            pl.BlockSpec((1, GATE_ROWS, tm), lambda b, i: (b, 0, i)),
        ],
        out_shape=[
            jax.ShapeDtypeStruct((n, T, ZA_W), BF16),
            jax.ShapeDtypeStruct((n, 1, T, 2 * C_KV * HEAD_DIM), BF16),
        ] + [
            jax.ShapeDtypeStruct((n, r, T // r, PAIR_OUT_W), BF16) for _, r in B_PAIRS
        ] + [
            jax.ShapeDtypeStruct((n, 2 * A_KV * VT_ROWS, T), BF16),
            jax.ShapeDtypeStruct((n, GATE_ROWS, T), F32),
        ],
        scratch_shapes=[pltpu.VMEM((PAIR_W // 128, tm, 128), F32)],
        compiler_params=_params("parallel", "parallel"),
        name="in_proj",
    )(x, g, wz, wtv, wtg)


def _compress_kernel(xk_ref, xv_ref, pos_ref, w1k_ref, b1k_ref, w2k_ref,
                     w1v_ref, b1v_ref, w2vt_ref, kc_ref, vct_ref):
    half = CMP_STRIDE * HEAD_DIM

    def hidden(x, w1_ref, b1_ref):
        y1 = _dot(x, w1_ref[0:half, :])
        y2 = _dot(x, w1_ref[half:2 * half, :])
        bias = _dot(pos_ref[...], w1_ref[...])[0:1] + b1_ref[...]
        nrow = y2.shape[0]
        return _gelu_tanh(y1 + pltpu.roll(y2, nrow - 1, 0) + bias).astype(BF16)

    kc_ref[0, 0] = _dot(hidden(xk_ref[0, 0], w1k_ref, b1k_ref), w2k_ref[...]).astype(BF16)
    vct_ref[0, 0] = _dot_nt(w2vt_ref[...], hidden(xv_ref[0, 0], w1v_ref, b1v_ref)).astype(BF16)


def _compress(xkv, pos, w1k, b1k, w2k, w1v, b1v, w2vt):
    n, _, nch, kdim = xkv.shape
    full = lambda shape: pl.BlockSpec(shape, lambda b, g: (0,) * len(shape))
    return pl.pallas_call(
        _compress_kernel,
        grid=(n, A_KV),
        in_specs=[
            pl.BlockSpec((1, 1, nch, kdim), lambda b, g: (b, g, 0, 0)),
            pl.BlockSpec((1, 1, nch, kdim), lambda b, g: (b, A_KV + g, 0, 0)),
            full(pos.shape), full(w1k.shape), full(b1k.shape), full(w2k.shape),
            full(w1v.shape), full(b1v.shape), full(w2vt.shape),
        ],
        out_specs=[
            pl.BlockSpec((1, 1, nch, HEAD_DIM), lambda b, g: (b, g, 0, 0)),
            pl.BlockSpec((1, 1, HEAD_DIM, nch), lambda b, g: (b, g, 0, 0)),
        ],
        out_shape=[
            jax.ShapeDtypeStruct((n, A_KV, nch, HEAD_DIM), BF16),
            jax.ShapeDtypeStruct((n, A_KV, HEAD_DIM, nch), BF16),
        ],
        compiler_params=_params("parallel", "parallel"),
        name="compress",
    )(xkv, xkv, pos, w1k, b1k, w2k, w1v, b1v, w2vt)


def _nsa_tables(T):
    sl = np.asarray(_SLOPES[A_SLOPE_OFF:A_SLOPE_OFF + A_HEADS], np.float32).reshape(A_KV, A_REP)
    slope = np.repeat(sl, BLOCK_Q, axis=1)[:, None, :]
    tl = np.tile(np.arange(BLOCK_Q, dtype=np.float32), A_REP)[None, None, :]
    rows = lambda n: np.arange(n, dtype=np.float32)[None, :, None]
    neg = np.float32(NEG_INF)
    p0 = slope * rows(SLC_LEN)
    pj = slope * (rows(T // SLC_LEN) * SLC_LEN)
    pc = slope * (rows(T // CMP_STRIDE) * CMP_STRIDE + np.float32((CMP_LEN - 1) / 2))
    kl = rows(BLOCK_Q)
    pd = np.where(kl <= tl, slope * (kl % SLC_LEN), neg)
    npb = -(-(WIN_A - 1) // BLOCK_Q)
    d = npb * BLOCK_Q + tl - rows((2 * npb + 1) * BLOCK_Q)
    bwin = np.where((d >= 0) & (d <= WIN_A - 1), -slope * d, neg)
    return [jnp.asarray(a.astype(np.float32)) for a in (p0, pj, pc, pd, bwin)]


def _nsa_kernel(q_ref, kc_ref, vct_ref, ks_ref, kw_ref, vt_ref, gt_ref, ovt_ref,
                p0_ref, pj_ref, pc_ref, pd_ref, bwin_ref, o_ref,
                selp_ref, s_ref, p_ref, acc_ref, m_ref, sw_ref, pw_ref):
    qb = pl.program_id(1)
    t0 = pl.multiple_of(qb * BLOCK_Q, BLOCK_Q)
    nl = A_REP * BLOCK_Q
    T = ks_ref.shape[1]
    ncmp = kc_ref.shape[2]
    nsel = ovt_ref.shape[0]
    lane = lax.broadcasted_iota(jnp.int32, (1, nl), 1)
    t_row = t0 + (lane & (BLOCK_Q - 1))
    sig = jax.nn.sigmoid(gt_ref[0])
    q = q_ref[0]
    blocks_per_q = BLOCK_Q // SLC_LEN

    groups = []
    for g in range(A_KV):
        heads = [g * A_REP + r for r in range(A_REP)]
        q3 = jnp.concatenate([q[:, h * HEAD_DIM:(h + 1) * HEAD_DIM] for h in heads], axis=0) * 0.125
        kcols = slice(g * HEAD_DIM, (g + 1) * HEAD_DIM)
        vs_rows = slice(g * VT_ROWS, (g + 1) * VT_ROWS)
        vw_rows = slice((A_KV + g) * VT_ROWS, (A_KV + g + 1) * VT_ROWS)

        st = _dot_nt(kc_ref[0, g], q3) + pc_ref[g]
        cst = lax.broadcasted_iota(jnp.int32, (ncmp, nl), 0) * CMP_STRIDE
        st = jnp.where((cst + (CMP_LEN - 1)) <= t_row, st, NEG_INF)
        m = jnp.max(st, axis=0, keepdims=True)
        m = jnp.where(m == NEG_INF, 0.0, m)
        p = jnp.exp(st - m)
        den = jnp.sum(p, axis=0, keepdims=True)
        pc = p / jnp.where(den > 0, den, 1.0)
        o_cmp = _dot(vct_ref[0, g], pc.astype(BF16))
        psum = pc[:, 0:BLOCK_Q] + pc[:, BLOCK_Q:2 * BLOCK_Q] + pc[:, 2 * BLOCK_Q:3 * BLOCK_Q]
        imp = _dot(ovt_ref[...], psum.astype(BF16))

        j = lax.broadcasted_iota(jnp.int32, (nsel, BLOCK_Q), 0)
        t128 = t0 + lax.broadcasted_iota(jnp.int32, (nsel, BLOCK_Q), 1)
        cur = t128 // SLC_LEN
        forced = (j == 0) | (j == cur) | (j == cur - 1)
        started = j * SLC_LEN <= t128
        score = jnp.where(started, jnp.where(forced, float("inf"), imp), NEG_INF)
        ahead = jnp.zeros((nsel, BLOCK_Q), jnp.int32)
        for k in range(nsel):
            rk = score[k:k + 1, :]
            ahead = ahead + ((rk > score) | ((rk == score) & (j > k))).astype(jnp.int32)
        keep = jnp.where((ahead < SLC_TOPN) & started, 0.0, NEG_INF)
        seln = jnp.concatenate([keep] * A_REP, axis=1) + pj_ref[g]
        jrow = lax.broadcasted_iota(jnp.int32, (nsel, nl), 0)
        selp_ref[g] = jnp.where(jrow < blocks_per_q * qb, seln, NEG_INF)

        sd = _dot_nt(ks_ref[0, pl.ds(t0, BLOCK_Q), kcols], q3) + pd_ref[g]
        per_block = pj_ref[g, 1:2, :]
        sd = sd + jnp.concatenate(
            [jnp.broadcast_to(per_block * (blocks_per_q * qb + b).astype(F32), (SLC_LEN, nl))
             for b in range(blocks_per_q)], axis=0)
        m_d = jnp.max(sd, axis=0, keepdims=True)
        m_ref[g] = m_d
        acc_ref[g] = jnp.zeros(acc_ref.shape[1:], F32)

        npb = -(-(WIN_A - 1) // BLOCK_Q)
        kwid = (npb + 1) * BLOCK_Q
        kst = pl.multiple_of(jnp.maximum(qb - npb, 0) * BLOCK_Q, BLOCK_Q)
        off = pl.multiple_of(kst + (npb - qb) * BLOCK_Q, BLOCK_Q)
        mx = None
        for c in range(0, kwid, BLOCK_Q):
            sc = (_dot_nt(kw_ref[0, pl.ds(kst + c, BLOCK_Q), kcols], q3)
                  + bwin_ref[g, pl.ds(off + c, BLOCK_Q), :])
            sw_ref[g, c:c + BLOCK_Q, :] = sc
            half = jnp.maximum(sc[0:BLOCK_Q // 2], sc[BLOCK_Q // 2:])
            mx = half if mx is None else jnp.maximum(mx, half)
        m = jnp.max(mx, axis=0, keepdims=True)
        for c in range(0, kwid, BLOCK_Q):
            pw_ref[g, c:c + BLOCK_Q, :] = jnp.exp(sw_ref[g, c:c + BLOCK_Q, :] - m).astype(BF16)
        acc = _dot(vt_ref[0, vw_rows, pl.ds(kst, kwid)], pw_ref[g])
        o_win = acc[0:HEAD_DIM] / acc[HEAD_DIM:HEAD_DIM + 1]
        groups.append((heads, q3, kcols, vs_rows, sd, m_d, o_cmp, o_win))

    def past(kv):
        for g, (heads, q3, kcols, vs_rows, sd, m_d, o_cmp, o_win) in enumerate(groups):
            p0 = p0_ref[g]
            mx = jnp.full((SLC_LEN, nl), NEG_INF, F32)
            for c0 in range(0, kv, SLC_CHUNK):
                s = _dot_nt(ks_ref[0, c0:c0 + SLC_CHUNK, kcols], q3)
                for b in range(SLC_CHUNK // SLC_LEN):
                    jb = c0 // SLC_LEN + b
                    sb = s[b * SLC_LEN:(b + 1) * SLC_LEN] + p0 + selp_ref[g, jb:jb + 1, :]
                    s_ref[g, jb * SLC_LEN:(jb + 1) * SLC_LEN, :] = sb
                    mx = jnp.maximum(mx, sb)
            m_all = jnp.maximum(m_d, jnp.max(mx, axis=0, keepdims=True))
            for c0 in range(0, kv, SLC_CHUNK):
                p_ref[g, c0:c0 + SLC_CHUNK, :] = jnp.exp(s_ref[g, c0:c0 + SLC_CHUNK, :] - m_all).astype(BF16)
            acc_ref[g] = _dot(vt_ref[0, vs_rows, 0:kv], p_ref[g, 0:kv, :])
            m_ref[g] = m_all

    q_per_chunk = SLC_CHUNK // BLOCK_Q
    for v in range(1, T // SLC_CHUNK + 1):
        pl.when((qb > q_per_chunk * (v - 1)) & (qb <= q_per_chunk * v))(functools.partial(past, v * SLC_CHUNK))

    for g, (heads, q3, kcols, vs_rows, sd, m_d, o_cmp, o_win) in enumerate(groups):
        p_d = jnp.exp(sd - m_ref[g]).astype(BF16)
        acc = acc_ref[g] + _dot(vt_ref[0, vs_rows, pl.ds(t0, BLOCK_Q)], p_d)
        o_slc = acc[0:HEAD_DIM] / acc[HEAD_DIM:HEAD_DIM + 1]

        for r, h in enumerate(heads):
            cols = slice(r * BLOCK_Q, (r + 1) * BLOCK_Q)
            o = (sig[3 * h:3 * h + 1] * o_cmp[:, cols] + sig[3 * h + 1:3 * h + 2] * o_slc[:, cols]
                 + sig[3 * h + 2:3 * h + 3] * o_win[:, cols])
            o_ref[0, h * HEAD_DIM:(h + 1) * HEAD_DIM, :] = o


def _nsa(z, kc, vct, vt, gt, ovt, tables):
    n, T, _ = z.shape
    ncmp = kc.shape[2]
    nsel = ovt.shape[0]
    kv_w = A_KV * HEAD_DIM
    nl = A_REP * BLOCK_Q
    win_rows = (-(-(WIN_A - 1) // BLOCK_Q) + 1) * BLOCK_Q
    const = lambda a: pl.BlockSpec(a.shape, lambda b, i: (0,) * a.ndim)
    return pl.pallas_call(
        _nsa_kernel,
        grid=(n, T // BLOCK_Q),
        in_specs=[
            pl.BlockSpec((1, BLOCK_Q, A_HEADS * HEAD_DIM), lambda b, i: (b, i, Z_AQ // (A_HEADS * HEAD_DIM))),
            pl.BlockSpec((1, A_KV, ncmp, HEAD_DIM), lambda b, i: (b, 0, 0, 0)),
            pl.BlockSpec((1, A_KV, HEAD_DIM, ncmp), lambda b, i: (b, 0, 0, 0)),
            pl.BlockSpec((1, T, kv_w), lambda b, i: (b, 0, Z_AKS // kv_w)),
            pl.BlockSpec((1, T, kv_w), lambda b, i: (b, 0, Z_AKW // kv_w)),
            pl.BlockSpec((1, 2 * A_KV * VT_ROWS, T), lambda b, i: (b, 0, 0)),
            pl.BlockSpec((1, GATE_ROWS, BLOCK_Q), lambda b, i: (b, 0, i)),
            const(ovt),
        ] + [const(a) for a in tables],
        out_specs=pl.BlockSpec((1, A_HEADS * HEAD_DIM, BLOCK_Q), lambda b, i: (b, 0, i)),
        out_shape=jax.ShapeDtypeStruct((n, A_HEADS * HEAD_DIM, T), F32),
        scratch_shapes=[
            pltpu.VMEM((A_KV, nsel, nl), F32),
            pltpu.VMEM((A_KV, T, nl), F32), pltpu.VMEM((A_KV, T, nl), BF16),
            pltpu.VMEM((A_KV, VT_ROWS, nl), F32), pltpu.VMEM((A_KV, 1, nl), F32),
            pltpu.VMEM((A_KV, win_rows, nl), F32), pltpu.VMEM((A_KV, win_rows, nl), BF16),
        ],
        compiler_params=_params("parallel", "parallel"),
        name="nsa",
    )(z, kc, vct, z, z, vt, gt, ovt, *tables)


def _band_table(slopes, dist_scale, past, kwid):
    npb = -(-past // BLOCK_Q)
    i = np.arange(BLOCK_Q)[:, None]
    j = np.arange(kwid + npb * BLOCK_Q)[None, :]
    diff = npb * BLOCK_Q + i - j
    sl = np.asarray(slopes, np.float32)[:, None, None]
    tab = np.where((diff >= 0) & (diff <= past), -sl * (diff * dist_scale).astype(np.float32), np.float32(NEG_INF))
    return jnp.asarray(tab.astype(np.float32))


def _banded_kernel(*refs, q_slabs, kv_slabs, head_of, past, kwid, n_sub, n_qb, has_sinks, want_lse):
    refs = list(refs)
    sink_ref = refs.pop(0) if has_sinks else None
    tab_ref, q_ref, k_ref, v_ref, o_ref = refs[:5]
    lse_ref = refs[5] if want_lse else None
    npb = -(-past // BLOCK_Q)
    cr = min(BLOCK_Q, 16 * 1024 // kwid)
    upper = lax.broadcasted_iota(jnp.int32, (cr, 128), 1) >= HEAD_DIM
    ones = jnp.ones((kwid, 128), BF16)
    for sub in range(n_sub):
        for part in range(n_qb * BLOCK_Q // cr):
            qb = pl.program_id(2) * n_qb + part * cr // BLOCK_Q
            kst = pl.multiple_of(jnp.maximum(qb - npb, 0) * BLOCK_Q, BLOCK_Q)
            off = pl.multiple_of(kst + (npb - qb) * BLOCK_Q, BLOCK_Q)
            rows = slice(part * cr, (part + 1) * cr)
            trow = slice(part * cr % BLOCK_Q, part * cr % BLOCK_Q + cr)
            for s_out in range(q_slabs):
                lanes = slice(s_out * 128, (s_out + 1) * 128)
                q = q_ref[0, sub, rows, lanes]
                kv = s_out * kv_slabs // q_slabs
                kwin = k_ref[0, sub, pl.ds(kst, kwid), kv * 128:(kv + 1) * 128]
                out, lses = [], []
                for half in range(2):
                    h = head_of[s_out][half]
                    qm = jnp.where(upper == (half == 1), q, jnp.zeros_like(q)) * 0.125
                    sr = _dot_nt(qm, kwin) + tab_ref[h, trow, pl.ds(off, kwid)]
                    m = jnp.max(sr, axis=1, keepdims=True)
                    if has_sinks:
                        m = jnp.maximum(m, sink_ref[h])
                    p = jnp.exp(sr - m).astype(BF16)
                    vpad = v_ref[0, sub, pl.ds(kst, kwid), (2 * kv + half) * 128:(2 * kv + half + 1) * 128]
                    den = _dot(p, ones)
                    if has_sinks:
                        den = den + jnp.exp(sink_ref[h] - m)
                    out.append(_dot(p, vpad) / den)
                    lses.append(m + jnp.log(den))
                o_ref[0, sub, rows, lanes] = (out[0] + out[1]).astype(o_ref.dtype)
                if want_lse:
                    lse_ref[0, sub, rows, lanes] = jnp.where(upper, lses[1], lses[0])


def _banded(q_arr, k_arr, v_arr, q_col, k_col, v_col, *, head_of, kv_heads, past, slopes, dist_scale,
            n_sub, n_qb, sinks, want_lse, out_dtype, name):
    n, n_seq, tf, _ = q_arr.shape
    qw, kw, vw = 128 * len(head_of), kv_heads * HEAD_DIM, 2 * kv_heads * HEAD_DIM
    kwid = min((-(-past // BLOCK_Q) + 1) * BLOCK_Q, tf)
    tab = _band_table(slopes, dist_scale, past, kwid)
    kern = functools.partial(_banded_kernel, q_slabs=len(head_of), kv_slabs=kw // 128, head_of=head_of,
                             past=past, kwid=kwid, n_sub=n_sub, n_qb=n_qb, has_sinks=sinks is not None,
                             want_lse=want_lse)
    rows = n_qb * BLOCK_Q
    out_spec = pl.BlockSpec((1, n_sub, rows, qw), lambda b, j, i: (b, j, i, 0))
    out_shape = jax.ShapeDtypeStruct((n, n_seq, tf, qw), out_dtype)
    in_specs = [
        pl.BlockSpec(tab.shape, lambda b, j, i: (0, 0, 0)),
        pl.BlockSpec((1, n_sub, rows, qw), lambda b, j, i: (b, j, i, q_col)),
        pl.BlockSpec((1, n_sub, tf, kw), lambda b, j, i: (b, j, 0, k_col)),
        pl.BlockSpec((1, n_sub, tf, vw), lambda b, j, i: (b, j, 0, v_col)),
    ]
    args = [tab, q_arr, k_arr, v_arr]
    if sinks is not None:
        in_specs.insert(0, pl.BlockSpec(memory_space=pltpu.SMEM))
        args.insert(0, sinks)
    return pl.pallas_call(
        kern,
        grid=(n, n_seq // n_sub, tf // rows),
        in_specs=in_specs,
        out_specs=[out_spec, out_spec] if want_lse else out_spec,
        out_shape=[out_shape, jax.ShapeDtypeStruct(out_shape.shape, F32)] if want_lse else out_shape,
        compiler_params=_params("parallel", "parallel", "parallel"),
        name=name,
    )(*args)


def _swa_sink(za, cv, sinks):
    n, T, _ = za.shape
    za4 = za.reshape(n, 1, T, ZA_W)
    return _banded(za4, za4, cv, Z_CQ // (C_HEADS * HEAD_DIM), Z_CK // (C_KV * HEAD_DIM), 0,
                   head_of=C_SLAB_HEADS, kv_heads=C_KV, past=WIN_C - 1, slopes=_SLOPES[:C_HEADS], dist_scale=1,
                   n_sub=1, n_qb=BAND_QB, sinks=sinks, want_lse=False, out_dtype=BF16, name="swa_sink")


def _dilated_pair(bp, pair):
    w, r = B_PAIRS[pair]
    tf = bp.shape[2]
    n_qb = min(BAND_QB, tf // BLOCK_Q)
    n_sub = min(r, BAND_QB // n_qb)
    hs = B_SLOPE_OFF + pair * B_HEADS
    head_of = tuple((2 * s, 2 * s + 1) for s in range(B_HEADS // 2))
    return _banded(bp, bp, bp, 0, 1, 1, head_of=head_of, kv_heads=B_HEADS, past=w // r,
                   slopes=_SLOPES[hs:hs + B_HEADS], dist_scale=r, n_sub=n_sub, n_qb=n_qb, sinks=None,
                   want_lse=True, out_dtype=BF16, name=f"dilated_{pair}")


def _out_proj_kernel(x_ref, oat_ref, ob0_ref, ob1_ref, ob2_ref, l0_ref, l1_ref, l2_ref,
                     oc_ref, w_ref, g_ref, y_ref, slab_ref):
    tm = x_ref.shape[1]
    n_slab = B_W // 128

    def unfold(ref, slot):
        r = ref.shape[1]
        if r == 1:
            return ref[0, 0].astype(F32)
        for j in range(r):
            for c in range(n_slab):
                slab_ref[slot * n_slab + c, pl.ds(j, tm // r, stride=r), :] = (
                    ref[0, j, :, c * 128:(c + 1) * 128].astype(F32))
        return jnp.concatenate([slab_ref[slot * n_slab + c] for c in range(n_slab)], axis=1)

    obs = [unfold(ref, i) for i, ref in enumerate((ob0_ref, ob1_ref, ob2_ref))]
    l0, l1, l2 = [unfold(ref, 3 + i) for i, ref in enumerate((l0_ref, l1_ref, l2_ref))]
    mx = jnp.maximum(jnp.maximum(l0, l1), l2)
    e0, e1, e2 = jnp.exp(l0 - mx), jnp.exp(l1 - mx), jnp.exp(l2 - mx)
    ob = (e0 * obs[0] + e1 * obs[1] + e2 * obs[2]) / (e0 + e1 + e2)
    wa = A_HEADS * HEAD_DIM
    wb = wa + B_HEADS * HEAD_DIM
    oa = oat_ref[0].T.astype(BF16)
    mix = (_dot(oa, w_ref[0:wa, :]) + _dot(ob.astype(BF16), w_ref[wa:wb, :])
           + _dot(oc_ref[0, 0], w_ref[wb:, :]))
    y_ref[0] = x_ref[0] + _rmsnorm(mix, g_ref[...])


def _out_proj(x, oat, obs, lses, oc, w, g, l):
    n, T, D = x.shape
    tm = TM_PROJ
    wa, wc = A_HEADS * HEAD_DIM, C_HEADS * HEAD_DIM
    row = lambda width: pl.BlockSpec((1, tm, width), lambda b, i: (b, i, 0))
    folded = lambda a: pl.BlockSpec((1, a.shape[1], tm // a.shape[1], a.shape[3]), lambda b, i: (b, 0, i, 0))
    return pl.pallas_call(
        _out_proj_kernel,
        grid=(n, T // tm),
        in_specs=[row(D), pl.BlockSpec((1, wa, tm), lambda b, i: (b, 0, i))]
        + [folded(a) for a in obs] + [folded(a) for a in lses]
        + [folded(oc), _layer(w, l), pl.BlockSpec((1, D), lambda b, i: (0, 0))],
        out_specs=row(D),
        out_shape=jax.ShapeDtypeStruct((n, T, D), F32),
        scratch_shapes=[pltpu.VMEM((2 * len(B_PAIRS) * (B_W // 128), tm, 128), F32)],
        compiler_params=_params("parallel", "parallel"),
        name="out_proj",
    )(x, oat, *obs, *lses, oc, w, g)


def _ffn_kernel(x_ref, xp_ref, gpre_ref, wu_ref, cw_ref, cb_ref, wd_ref, gpost_ref, y_ref, h_ref, act_ref):
    ti = pl.program_id(1)
    tf = TF_FFN
    hp = _rmsnorm(xp_ref[0], gpre_ref[...])
    h_ref[0:FFN_HALO, :] = jnp.where(ti > 0, hp, 0.0).astype(BF16)
    h_ref[FFN_HALO:, :] = _rmsnorm(x_ref[0], gpre_ref[...]).astype(BF16)
    h = h_ref[...]

    def conv_branch(c0):
        u = _dot(h, wu_ref[:, c0:c0 + tf])
        cw = cw_ref[:, c0:c0 + tf]
        y = cw[2:3] * u + cw[1:2] * pltpu.roll(u, 1, 0) + cw[0:1] * pltpu.roll(u, 2, 0)
        return y[FFN_HALO:] + cb_ref[:, c0:c0 + tf]

    nff = D_FF // tf
    bounds = [nff * (i + 1) // FFN_DOWN_CHUNKS for i in range(FFN_DOWN_CHUNKS)]
    done = 0
    for k in range(nff):
        a = conv_branch(k * tf)
        b = conv_branch(D_FF + k * tf)
        act_ref[:, k * tf:(k + 1) * tf] = (_gelu_tanh(a) * b).astype(BF16)
        if k + 1 in bounds:
            lo, hi = done * tf, (k + 1) * tf
            part = _dot(act_ref[:, lo:hi], wd_ref[lo:hi, :])
            if k + 1 == nff:
                total = part if done == 0 else y_ref[0] + part
                y_ref[0] = x_ref[0] + _rmsnorm(total, gpost_ref[...])
            elif done == 0:
                y_ref[0] = part
            else:
                y_ref[0] += part
            done = k + 1


def _ffn(x, gpre, w_up, conv_w, conv_b, w_down, gpost, l):
    n, T, D = x.shape
    tm = TM_FFN
    halo_blocks = tm // FFN_HALO
    resident = lambda a: (_layer(a, l, pipeline_mode=pl.Buffered(1)) if a.ndim == 3 else
                          pl.BlockSpec(a.shape, lambda b, i: (0,) * a.ndim, pipeline_mode=pl.Buffered(1)))
    return pl.pallas_call(
        _ffn_kernel,
        grid=(n, T // tm),
        in_specs=[
            pl.BlockSpec((1, tm, D), lambda b, i: (b, i, 0)),
            pl.BlockSpec((1, FFN_HALO, D), lambda b, i: (b, jnp.maximum(i * halo_blocks - 1, 0), 0)),
            resident(gpre), resident(w_up), resident(conv_w), resident(conv_b), resident(w_down), resident(gpost),
        ],
        out_specs=pl.BlockSpec((1, tm, D), lambda b, i: (b, i, 0)),
        out_shape=jax.ShapeDtypeStruct((n, T, D), F32),
        scratch_shapes=[pltpu.VMEM((FFN_HALO + tm, D), BF16), pltpu.VMEM((tm, D_FF), BF16)],
        compiler_params=_params("parallel", "parallel"),
        name="ffn",
    )(x, x, gpre, w_up, conv_w, conv_b, w_down, gpost)


def _block_overlap_t(nc, ns):
    cs = np.arange(nc)[None, :] * CMP_STRIDE
    ss = np.arange(ns)[:, None] * SLC_LEN
    ov = np.minimum(cs + CMP_LEN, ss + SLC_LEN) - np.maximum(cs, ss)
    return (np.clip(ov, 0, None) / CMP_STRIDE).astype(np.float32)


def kernel(x, g_mix_pre, g_mix_post, g_ffn_pre, g_ffn_post, w_in, cmp_pos, cmp_k_w1, cmp_k_b1, cmp_k_w2,
           cmp_v_w1, cmp_v_b1, cmp_v_w2, sinks, w_out, w_up, conv_w, conv_b, w_down):
    n, T, _ = x.shape
    depth = w_in.shape[0]
    nch = T // CMP_STRIDE
    assert T % (BLOCK_Q * B_PAIRS[-1][1]) == 0 and T % TM_FFN == 0 and nch % BLOCK_Q == 0

    def cols(i):
        return w_in[:, :, IN_OFFS[i]:IN_OFFS[i + 1]]

    (aq, akc, avc, aks, avs, akw, avw, agate, bq, bk, bv, cq, ck, cv) = [cols(i) for i in range(len(IN_SIZES))]
    order = jnp.asarray(C_HEAD_ORDER)
    cq = cq.reshape(depth, D_MODEL, C_HEADS, HEAD_DIM)[:, :, order].reshape(depth, D_MODEL, C_HEADS * HEAD_DIM)
    c_row0 = (A_HEADS + B_HEADS) * HEAD_DIM
    w_out = jnp.concatenate(
        [w_out[:, :c_row0], w_out[:, c_row0:].reshape(depth, C_HEADS, HEAD_DIM, D_MODEL)[:, order].reshape(depth, -1, D_MODEL)],
        axis=1)
    pair_cols = [a[:, :, p * B_W:(p + 1) * B_W] for p in range(len(B_PAIRS)) for a in (bq, bk, bv)]
    wz = jnp.concatenate([aq, cq, akc, avc, aks, akw, ck, cv] + pair_cols, axis=-1).astype(BF16)
    wtv = jnp.concatenate([avs, avw], axis=-1).transpose(0, 2, 1).astype(BF16)
    wtg = jnp.pad(agate, ((0, 0), (0, 0), (0, GATE_ROWS - agate.shape[-1]))).transpose(0, 2, 1).astype(BF16)
    pos = jnp.broadcast_to(cmp_pos.reshape(depth, 1, CMP_LEN * HEAD_DIM), (depth, 8, CMP_LEN * HEAD_DIM)).astype(BF16)
    w1k, w1v = cmp_k_w1.astype(BF16), cmp_v_w1.astype(BF16)
    w2k = cmp_k_w2.astype(BF16)
    w2vt = cmp_v_w2.transpose(0, 2, 1).astype(BF16)
    w_out_b, w_up_b, w_down_b = w_out.astype(BF16), w_up.astype(BF16), w_down.astype(BF16)
    ovt = jnp.asarray(_block_overlap_t(nch, T // SLC_LEN), BF16)

    tables = _nsa_tables(T)
    row = lambda a: a.reshape(1, -1)
    for l in range(depth):
        z, cvp, b0, b1, b2, vt, gt = _in_proj(x, row(g_mix_pre[l]), wz, wtv, wtg, l)
        xkv = (z[:, :, Z_AKC:Z_AKC + 2 * A_KV * HEAD_DIM]
               .reshape(n, nch, CMP_STRIDE, 2 * A_KV, HEAD_DIM)
               .transpose(0, 3, 1, 2, 4)
               .reshape(n, 2 * A_KV, nch, CMP_STRIDE * HEAD_DIM))
        kc, vct = _compress(xkv, pos[l], w1k[l], row(cmp_k_b1[l]), w2k[l], w1v[l], row(cmp_v_b1[l]), w2vt[l])
        oat = _nsa(z, kc, vct, vt, gt, ovt, tables)
        oc = _swa_sink(z, cvp, sinks[l])
        pairs = [_dilated_pair(bp, p) for p, bp in enumerate((b0, b1, b2))]
        x = _out_proj(x, oat, [o for o, _ in pairs], [s for _, s in pairs], oc, w_out_b, row(g_mix_post[l]), l)
        x = _ffn(x, row(g_ffn_pre[l]), w_up_b, conv_w, row(conv_b[l]), w_down_b, row(g_ffn_post[l]), l)
    return x
```

```python
import functools

import numpy as np
import jax
import jax.numpy as jnp
from jax import lax
from jax.experimental import pallas as pl
from jax.experimental.pallas import tpu as pltpu

D_MODEL = 1024
HEAD_DIM = 64
BLOCK_Q = 128
A_KV, A_REP = 2, 3
A_HEADS = A_KV * A_REP
CMP_LEN, CMP_STRIDE, CMP_HIDDEN = 32, 16, 256
SLC_LEN, SLC_TOPN = 64, 8
WIN_A = 512
B_PAIRS = ((128, 1), (512, 4), (2048, 16))
B_HEADS = 4
C_KV, C_REP = 2, 3
C_HEADS = C_KV * C_REP
WIN_C = 128
D_FF = 2816
CONV_W = 3
EPS = 1e-6

N_ALIBI = C_HEADS + len(B_PAIRS) * B_HEADS + A_HEADS
B_SLOPE_OFF = C_HEADS
A_SLOPE_OFF = C_HEADS + len(B_PAIRS) * B_HEADS
IN_SIZES = ((A_HEADS * HEAD_DIM,) + (A_KV * HEAD_DIM,) * 6 + (A_HEADS * 3,)
            + (len(B_PAIRS) * B_HEADS * HEAD_DIM,) * 3
            + (C_HEADS * HEAD_DIM, C_KV * HEAD_DIM, C_KV * HEAD_DIM))
IN_OFFS = tuple(int(v) for v in np.cumsum((0,) + IN_SIZES))

_SLOPES = tuple(float(v) for v in
                np.exp2(-8.0 * np.arange(1, N_ALIBI + 1, dtype=np.float32) / N_ALIBI).astype(np.float32))

Z_AQ, Z_CQ = 0, 384
Z_AKC, Z_AVC, Z_AKS, Z_AKW, Z_CK = 768, 896, 1024, 1152, 1280
ZA_W = 1408
W_CV = 1408
W_B = 1536
B_W = B_HEADS * HEAD_DIM
PAIR_W = 3 * B_W
PAIR_OUT_W = 2 * B_W + 2 * B_W
W_ALL = W_B + len(B_PAIRS) * PAIR_W
GATE_ROWS = 32
C_SLAB_HEADS = tuple((r, C_REP + r) for r in range(C_REP))
C_HEAD_ORDER = tuple(h for pair in C_SLAB_HEADS for h in pair)

F32 = jnp.float32
BF16 = jnp.bfloat16
NEG_INF = float("-inf")

TM_PROJ = 512
TM_IN = 1024
TM_FFN = 1024
TF_FFN = 256
FFN_HALO = 16
FFN_DOWN_CHUNKS = 3
BAND_QB = 4
SLC_CHUNK = 512
VT_ROWS = 80
VMEM_LIMIT = 56 * 1024 * 1024


def _dot(a, b):
    return jnp.dot(a, b, preferred_element_type=F32)


def _dot_nt(a, b):
    return lax.dot_general(a, b, (((1,), (1,)), ((), ())), preferred_element_type=F32)


def _rmsnorm(x, g):
    return x * lax.rsqrt(jnp.mean(x * x, axis=-1, keepdims=True) + EPS) * g


def _gelu_tanh(x):
    return 0.5 * x * (1.0 + jnp.tanh(0.7978845608028654 * (x + 0.044715 * (x * x * x))))


def _params(*sem):
    return pltpu.CompilerParams(dimension_semantics=sem, vmem_limit_bytes=VMEM_LIMIT)


def _layer(a, l, **kw):
    return pl.BlockSpec((None,) + a.shape[1:], lambda *_: (l,) + (0,) * (a.ndim - 1), **kw)


def _pad_values(v):
    upper = lax.broadcasted_iota(jnp.int32, (v.shape[0], 128), 1) >= HEAD_DIM
    parts = []
    for s in range(v.shape[1] // 128):
        slab = v[:, s * 128:(s + 1) * 128].astype(BF16)
        parts += [jnp.where(upper, jnp.zeros_like(slab), slab), jnp.where(upper, slab, jnp.zeros_like(slab))]
    return jnp.concatenate(parts, axis=1)


def _in_proj_kernel(x_ref, g_ref, w_ref, wtv_ref, wtg_ref, za_ref, cv_ref, b0_ref, b1_ref, b2_ref,
                    vt_ref, gt_ref, xkv_ref, slab_ref):
    h = _rmsnorm(x_ref[0], g_ref[...]).astype(BF16)
    tm = h.shape[0]
    za_ref[0, :, 0:Z_AKC] = _dot(h, w_ref[:, 0:Z_AKC]).astype(BF16)
    rest = _dot(h, w_ref[:, Z_AKC:ZA_W])
    za_ref[0, :, Z_AKC:ZA_W] = rest.astype(BF16)
    n_cmp = 2 * A_KV * HEAD_DIM // 128
    for s in range(n_cmp):
        slab_ref[s] = rest[:, s * 128:(s + 1) * 128]
    taps = [[slab_ref[s, pl.ds(p, tm // CMP_STRIDE, stride=CMP_STRIDE), :] for p in range(CMP_STRIDE)]
            for s in range(n_cmp)]
    for c in range(2 * A_KV):
        lanes = slice((c % 2) * HEAD_DIM, (c % 2 + 1) * HEAD_DIM)
        xkv_ref[0, c] = jnp.concatenate([t[:, lanes] for t in taps[c // 2]], axis=1).astype(BF16)
    cv_ref[0, 0] = _pad_values(_dot(h, w_ref[:, W_CV:W_B]))
    n_slab = PAIR_W // 128
    for p, b_ref in enumerate((b0_ref, b1_ref, b2_ref)):
        r = B_PAIRS[p][1]
        res = _dot(h, w_ref[:, W_B + p * PAIR_W:W_B + (p + 1) * PAIR_W])
        if r == 1:
            b_ref[0, 0] = jnp.concatenate([res[:, 0:2 * B_W].astype(BF16), _pad_values(res[:, 2 * B_W:])], axis=1)
            continue
        for c in range(n_slab):
            slab_ref[c] = res[:, c * 128:(c + 1) * 128]
        for j in range(r):
            rows = jnp.concatenate([slab_ref[c, pl.ds(j, tm // r, stride=r), :] for c in range(n_slab)], axis=1)
            b_ref[0, j] = jnp.concatenate([rows[:, 0:2 * B_W].astype(BF16), _pad_values(rows[:, 2 * B_W:])], axis=1)
    vt = _dot_nt(wtv_ref[...], h).astype(BF16)
    fill = VT_ROWS - HEAD_DIM
    ones_row = (lax.broadcasted_iota(jnp.int32, (fill, vt.shape[1]), 0) == 0).astype(BF16)
    for i in range(2 * A_KV):
        vt_ref[0, i * VT_ROWS:i * VT_ROWS + HEAD_DIM, :] = vt[i * HEAD_DIM:(i + 1) * HEAD_DIM]
        vt_ref[0, i * VT_ROWS + HEAD_DIM:(i + 1) * VT_ROWS, :] = ones_row
    gt_ref[0] = _dot_nt(wtg_ref[...], h)


def _in_proj(x, g, wz, wtv, wtg, l):
    n, T, D = x.shape
    tm = TM_IN
    return pl.pallas_call(
        _in_proj_kernel,
        grid=(n, T // tm),
        in_specs=[
            pl.BlockSpec((1, tm, D), lambda b, i: (b, i, 0)),
            pl.BlockSpec((1, D), lambda b, i: (0, 0)),
            _layer(wz, l, pipeline_mode=pl.Buffered(1)), _layer(wtv, l), _layer(wtg, l),
        ],
        out_specs=[
            pl.BlockSpec((1, tm, ZA_W), lambda b, i: (b, i, 0)),
            pl.BlockSpec((1, 1, tm, 2 * C_KV * HEAD_DIM), lambda b, i: (b, 0, i, 0)),
        ] + [
            pl.BlockSpec((1, r, tm // r, PAIR_OUT_W), lambda b, i: (b, 0, i, 0)) for _, r in B_PAIRS
        ] + [
            pl.BlockSpec((1, 2 * A_KV * VT_ROWS, tm), lambda b, i: (b, 0, i)),
            pl.BlockSpec((1, GATE_ROWS, tm), lambda b, i: (b, 0, i)),
            pl.BlockSpec((1, 2 * A_KV, tm // CMP_STRIDE, CMP_STRIDE * HEAD_DIM), lambda b, i: (b, 0, i, 0)),
        ],
        out_shape=[
            jax.ShapeDtypeStruct((n, T, ZA_W), BF16),
            jax.ShapeDtypeStruct((n, 1, T, 2 * C_KV * HEAD_DIM), BF16),
        ] + [
            jax.ShapeDtypeStruct((n, r, T // r, PAIR_OUT_W), BF16) for _, r in B_PAIRS
        ] + [
            jax.ShapeDtypeStruct((n, 2 * A_KV * VT_ROWS, T), BF16),
            jax.ShapeDtypeStruct((n, GATE_ROWS, T), F32),
            jax.ShapeDtypeStruct((n, 2 * A_KV, T // CMP_STRIDE, CMP_STRIDE * HEAD_DIM), BF16),
        ],
        scratch_shapes=[pltpu.VMEM((PAIR_W // 128, tm, 128), F32)],
        compiler_params=_params("parallel", "parallel"),
        name="in_proj",
    )(x, g, wz, wtv, wtg)


def _compress_kernel(xk_ref, xv_ref, pos_ref, w1k_ref, b1k_ref, w2k_ref,
                     w1v_ref, b1v_ref, w2vt_ref, kc_ref, vct_ref):
    half = CMP_STRIDE * HEAD_DIM

    def hidden(x, w1_ref, b1_ref):
        y1 = _dot(x, w1_ref[0:half, :])
        y2 = _dot(x, w1_ref[half:2 * half, :])
        bias = _dot(pos_ref[...], w1_ref[...])[0:1] + b1_ref[...]
        nrow = y2.shape[0]
        return _gelu_tanh(y1 + pltpu.roll(y2, nrow - 1, 0) + bias).astype(BF16)

    kc_ref[0, 0] = _dot(hidden(xk_ref[0, 0], w1k_ref, b1k_ref), w2k_ref[...]).astype(BF16)
    vct_ref[0, 0] = _dot_nt(w2vt_ref[...], hidden(xv_ref[0, 0], w1v_ref, b1v_ref)).astype(BF16)


def _compress(xkv, pos, w1k, b1k, w2k, w1v, b1v, w2vt):
    n, _, nch, kdim = xkv.shape
    full = lambda shape: pl.BlockSpec(shape, lambda b, g: (0,) * len(shape))
    return pl.pallas_call(
        _compress_kernel,
        grid=(n, A_KV),
        in_specs=[
            pl.BlockSpec((1, 1, nch, kdim), lambda b, g: (b, g, 0, 0)),
            pl.BlockSpec((1, 1, nch, kdim), lambda b, g: (b, A_KV + g, 0, 0)),
            full(pos.shape), full(w1k.shape), full(b1k.shape), full(w2k.shape),
            full(w1v.shape), full(b1v.shape), full(w2vt.shape),
        ],
        out_specs=[
            pl.BlockSpec((1, 1, nch, HEAD_DIM), lambda b, g: (b, g, 0, 0)),
            pl.BlockSpec((1, 1, HEAD_DIM, nch), lambda b, g: (b, g, 0, 0)),
        ],
        out_shape=[
            jax.ShapeDtypeStruct((n, A_KV, nch, HEAD_DIM), BF16),
            jax.ShapeDtypeStruct((n, A_KV, HEAD_DIM, nch), BF16),
        ],
        compiler_params=_params("parallel", "parallel"),
        name="compress",
    )(xkv, xkv, pos, w1k, b1k, w2k, w1v, b1v, w2vt)


def _nsa_tables(T):
    sl = np.asarray(_SLOPES[A_SLOPE_OFF:A_SLOPE_OFF + A_HEADS], np.float32).reshape(A_KV, A_REP)
    slope = np.repeat(sl, BLOCK_Q, axis=1)[:, None, :]
    tl = np.tile(np.arange(BLOCK_Q, dtype=np.float32), A_REP)[None, None, :]
    rows = lambda n: np.arange(n, dtype=np.float32)[None, :, None]
    neg = np.float32(NEG_INF)
    p0 = slope * rows(SLC_LEN)
    pj = slope * (rows(T // SLC_LEN) * SLC_LEN)
    pc = slope * (rows(T // CMP_STRIDE) * CMP_STRIDE + np.float32((CMP_LEN - 1) / 2))
    kl = rows(BLOCK_Q)
    pd = np.where(kl <= tl, slope * (kl % SLC_LEN), neg)
    npb = -(-(WIN_A - 1) // BLOCK_Q)
    d = npb * BLOCK_Q + tl - rows((2 * npb + 1) * BLOCK_Q)
    bwin = np.where((d >= 0) & (d <= WIN_A - 1), -slope * d, neg)
    return [jnp.asarray(a.astype(np.float32)) for a in (p0, pj, pc, pd, bwin)]


def _nsa_kernel(q_ref, kc_ref, vct_ref, ks_ref, kw_ref, vt_ref, gt_ref, ovt_ref,
                p0_ref, pj_ref, pc_ref, pd_ref, bwin_ref, o_ref,
                selp_ref, s_ref, p_ref, acc_ref, m_ref, sw_ref, pw_ref):
    qb = pl.program_id(1)
    t0 = pl.multiple_of(qb * BLOCK_Q, BLOCK_Q)
    nl = A_REP * BLOCK_Q
    T = ks_ref.shape[1]
    ncmp = kc_ref.shape[2]
    nsel = ovt_ref.shape[0]
    lane = lax.broadcasted_iota(jnp.int32, (1, nl), 1)
    t_row = t0 + (lane & (BLOCK_Q - 1))
    sig = jax.nn.sigmoid(gt_ref[0])
    q = q_ref[0]
    blocks_per_q = BLOCK_Q // SLC_LEN

    groups = []
    for g in range(A_KV):
        heads = [g * A_REP + r for r in range(A_REP)]
        q3 = jnp.concatenate([q[:, h * HEAD_DIM:(h + 1) * HEAD_DIM] for h in heads], axis=0) * 0.125
        kcols = slice(g * HEAD_DIM, (g + 1) * HEAD_DIM)
        vs_rows = slice(g * VT_ROWS, (g + 1) * VT_ROWS)
        vw_rows = slice((A_KV + g) * VT_ROWS, (A_KV + g + 1) * VT_ROWS)

        st = _dot_nt(kc_ref[0, g], q3) + pc_ref[g]
        cst = lax.broadcasted_iota(jnp.int32, (ncmp, nl), 0) * CMP_STRIDE
        st = jnp.where((cst + (CMP_LEN - 1)) <= t_row, st, NEG_INF)
        m = jnp.max(st, axis=0, keepdims=True)
        m = jnp.where(m == NEG_INF, 0.0, m)
        p = jnp.exp(st - m)
        den = jnp.sum(p, axis=0, keepdims=True)
        pc = p / jnp.where(den > 0, den, 1.0)
        o_cmp = _dot(vct_ref[0, g], pc.astype(BF16))
        psum = pc[:, 0:BLOCK_Q] + pc[:, BLOCK_Q:2 * BLOCK_Q] + pc[:, 2 * BLOCK_Q:3 * BLOCK_Q]
        imp = _dot(ovt_ref[...], psum.astype(BF16))

        j = lax.broadcasted_iota(jnp.int32, (nsel, BLOCK_Q), 0)
        t128 = t0 + lax.broadcasted_iota(jnp.int32, (nsel, BLOCK_Q), 1)
        cur = t128 // SLC_LEN
        forced = (j == 0) | (j == cur) | (j == cur - 1)
        started = j * SLC_LEN <= t128
        score = jnp.where(started, jnp.where(forced, float("inf"), imp), NEG_INF)
        ahead = jnp.zeros((nsel, BLOCK_Q), jnp.int32)
        for k in range(nsel):
            rk = score[k:k + 1, :]
            ahead = ahead + ((rk > score) | ((rk == score) & (j > k))).astype(jnp.int32)
        keep = jnp.where((ahead < SLC_TOPN) & started, 0.0, NEG_INF)
        seln = jnp.concatenate([keep] * A_REP, axis=1) + pj_ref[g]
        jrow = lax.broadcasted_iota(jnp.int32, (nsel, nl), 0)
        selp_ref[g] = jnp.where(jrow < blocks_per_q * qb, seln, NEG_INF)

        sd = _dot_nt(ks_ref[0, pl.ds(t0, BLOCK_Q), kcols], q3) + pd_ref[g]
        per_block = pj_ref[g, 1:2, :]
        sd = sd + jnp.concatenate(
            [jnp.broadcast_to(per_block * (blocks_per_q * qb + b).astype(F32), (SLC_LEN, nl))
             for b in range(blocks_per_q)], axis=0)
        m_d = jnp.max(sd, axis=0, keepdims=True)
        m_ref[g] = m_d
        acc_ref[g] = jnp.zeros(acc_ref.shape[1:], F32)

        npb = -(-(WIN_A - 1) // BLOCK_Q)
        kwid = (npb + 1) * BLOCK_Q
        kst = pl.multiple_of(jnp.maximum(qb - npb, 0) * BLOCK_Q, BLOCK_Q)
        off = pl.multiple_of(kst + (npb - qb) * BLOCK_Q, BLOCK_Q)
        mx = None
        for c in range(0, kwid, BLOCK_Q):
            sc = (_dot_nt(kw_ref[0, pl.ds(kst + c, BLOCK_Q), kcols], q3)
                  + bwin_ref[g, pl.ds(off + c, BLOCK_Q), :])
            sw_ref[g, c:c + BLOCK_Q, :] = sc
            half = jnp.maximum(sc[0:BLOCK_Q // 2], sc[BLOCK_Q // 2:])
            mx = half if mx is None else jnp.maximum(mx, half)
        m = jnp.max(mx, axis=0, keepdims=True)
        for c in range(0, kwid, BLOCK_Q):
            pw_ref[g, c:c + BLOCK_Q, :] = jnp.exp(sw_ref[g, c:c + BLOCK_Q, :] - m).astype(BF16)
        acc = _dot(vt_ref[0, vw_rows, pl.ds(kst, kwid)], pw_ref[g])
        o_win = acc[0:HEAD_DIM] / acc[HEAD_DIM:HEAD_DIM + 1]
        groups.append((heads, q3, kcols, vs_rows, sd, m_d, o_cmp, o_win))

    def past(kv):
        for g, (heads, q3, kcols, vs_rows, sd, m_d, o_cmp, o_win) in enumerate(groups):
            p0 = p0_ref[g]
            mx = jnp.full((SLC_LEN, nl), NEG_INF, F32)
            for c0 in range(0, kv, SLC_CHUNK):
                s = _dot_nt(ks_ref[0, c0:c0 + SLC_CHUNK, kcols], q3)
                for b in range(SLC_CHUNK // SLC_LEN):
                    jb = c0 // SLC_LEN + b
                    sb = s[b * SLC_LEN:(b + 1) * SLC_LEN] + p0 + selp_ref[g, jb:jb + 1, :]
                    s_ref[g, jb * SLC_LEN:(jb + 1) * SLC_LEN, :] = sb
                    mx = jnp.maximum(mx, sb)
            m_all = jnp.maximum(m_d, jnp.max(mx, axis=0, keepdims=True))
            for c0 in range(0, kv, SLC_CHUNK):
                p_ref[g, c0:c0 + SLC_CHUNK, :] = jnp.exp(s_ref[g, c0:c0 + SLC_CHUNK, :] - m_all).astype(BF16)
            acc_ref[g] = _dot(vt_ref[0, vs_rows, 0:kv], p_ref[g, 0:kv, :])
            m_ref[g] = m_all

    q_per_chunk = SLC_CHUNK // BLOCK_Q
    for v in range(1, T // SLC_CHUNK + 1):
        pl.when((qb > q_per_chunk * (v - 1)) & (qb <= q_per_chunk * v))(functools.partial(past, v * SLC_CHUNK))

    for g, (heads, q3, kcols, vs_rows, sd, m_d, o_cmp, o_win) in enumerate(groups):
        p_d = jnp.exp(sd - m_ref[g]).astype(BF16)
        acc = acc_ref[g] + _dot(vt_ref[0, vs_rows, pl.ds(t0, BLOCK_Q)], p_d)
        o_slc = acc[0:HEAD_DIM] / acc[HEAD_DIM:HEAD_DIM + 1]

        for r, h in enumerate(heads):
            cols = slice(r * BLOCK_Q, (r + 1) * BLOCK_Q)
            o = (sig[3 * h:3 * h + 1] * o_cmp[:, cols] + sig[3 * h + 1:3 * h + 2] * o_slc[:, cols]
                 + sig[3 * h + 2:3 * h + 3] * o_win[:, cols])
            o_ref[0, h * HEAD_DIM:(h + 1) * HEAD_DIM, :] = o


def _nsa(z, kc, vct, vt, gt, ovt, tables):
    n, T, _ = z.shape
    ncmp = kc.shape[2]
    nsel = ovt.shape[0]
    kv_w = A_KV * HEAD_DIM
    nl = A_REP * BLOCK_Q
    win_rows = (-(-(WIN_A - 1) // BLOCK_Q) + 1) * BLOCK_Q
    const = lambda a: pl.BlockSpec(a.shape, lambda b, i: (0,) * a.ndim)
    return pl.pallas_call(
        _nsa_kernel,
        grid=(n, T // BLOCK_Q),
        in_specs=[
            pl.BlockSpec((1, BLOCK_Q, A_HEADS * HEAD_DIM), lambda b, i: (b, i, Z_AQ // (A_HEADS * HEAD_DIM))),
            pl.BlockSpec((1, A_KV, ncmp, HEAD_DIM), lambda b, i: (b, 0, 0, 0)),
            pl.BlockSpec((1, A_KV, HEAD_DIM, ncmp), lambda b, i: (b, 0, 0, 0)),
            pl.BlockSpec((1, T, kv_w), lambda b, i: (b, 0, Z_AKS // kv_w)),
            pl.BlockSpec((1, T, kv_w), lambda b, i: (b, 0, Z_AKW // kv_w)),
            pl.BlockSpec((1, 2 * A_KV * VT_ROWS, T), lambda b, i: (b, 0, 0)),
            pl.BlockSpec((1, GATE_ROWS, BLOCK_Q), lambda b, i: (b, 0, i)),
            const(ovt),
        ] + [const(a) for a in tables],
        out_specs=pl.BlockSpec((1, A_HEADS * HEAD_DIM, BLOCK_Q), lambda b, i: (b, 0, i)),
        out_shape=jax.ShapeDtypeStruct((n, A_HEADS * HEAD_DIM, T), F32),
        scratch_shapes=[
            pltpu.VMEM((A_KV, nsel, nl), F32),
            pltpu.VMEM((A_KV, T, nl), F32), pltpu.VMEM((A_KV, T, nl), BF16),
            pltpu.VMEM((A_KV, VT_ROWS, nl), F32), pltpu.VMEM((A_KV, 1, nl), F32),
            pltpu.VMEM((A_KV, win_rows, nl), F32), pltpu.VMEM((A_KV, win_rows, nl), BF16),
        ],
        compiler_params=_params("parallel", "parallel"),
        name="nsa",
    )(z, kc, vct, z, z, vt, gt, ovt, *tables)


def _band_table(slopes, dist_scale, past, kwid):
    npb = -(-past // BLOCK_Q)
    i = np.arange(BLOCK_Q)[:, None]
    j = np.arange(kwid + npb * BLOCK_Q)[None, :]
    diff = npb * BLOCK_Q + i - j
    sl = np.asarray(slopes, np.float32)[:, None, None]
    tab = np.where((diff >= 0) & (diff <= past), -sl * (diff * dist_scale).astype(np.float32), np.float32(NEG_INF))
    return jnp.asarray(tab.astype(np.float32))


def _banded_kernel(*refs, q_slabs, kv_slabs, head_of, past, kwid, n_sub, n_qb, has_sinks, want_lse):
    refs = list(refs)
    sink_ref = refs.pop(0) if has_sinks else None
    tab_ref, q_ref, k_ref, v_ref, o_ref = refs[:5]
    lse_ref = refs[5] if want_lse else None
    npb = -(-past // BLOCK_Q)
    cr = min(BLOCK_Q, 16 * 1024 // kwid)
    upper = lax.broadcasted_iota(jnp.int32, (cr, 128), 1) >= HEAD_DIM
    ones = jnp.ones((kwid, 128), BF16)
    for sub in range(n_sub):
        for part in range(n_qb * BLOCK_Q // cr):
            qb = pl.program_id(2) * n_qb + part * cr // BLOCK_Q
            kst = pl.multiple_of(jnp.maximum(qb - npb, 0) * BLOCK_Q, BLOCK_Q)
            off = pl.multiple_of(kst + (npb - qb) * BLOCK_Q, BLOCK_Q)
            rows = slice(part * cr, (part + 1) * cr)
            trow = slice(part * cr % BLOCK_Q, part * cr % BLOCK_Q + cr)
            for s_out in range(q_slabs):
                lanes = slice(s_out * 128, (s_out + 1) * 128)
                q = q_ref[0, sub, rows, lanes]
                kv = s_out * kv_slabs // q_slabs
                kwin = k_ref[0, sub, pl.ds(kst, kwid), kv * 128:(kv + 1) * 128]
                out, lses = [], []
                for half in range(2):
                    h = head_of[s_out][half]
                    qm = jnp.where(upper == (half == 1), q, jnp.zeros_like(q)) * 0.125
                    sr = _dot_nt(qm, kwin) + tab_ref[h, trow, pl.ds(off, kwid)]
                    m = jnp.max(sr, axis=1, keepdims=True)
                    if has_sinks:
                        m = jnp.maximum(m, sink_ref[h])
                    p = jnp.exp(sr - m).astype(BF16)
                    vpad = v_ref[0, sub, pl.ds(kst, kwid), (2 * kv + half) * 128:(2 * kv + half + 1) * 128]
                    den = _dot(p, ones)
                    if has_sinks:
                        den = den + jnp.exp(sink_ref[h] - m)
                    out.append(_dot(p, vpad) / den)
                    lses.append(m + jnp.log(den))
                o_ref[0, sub, rows, lanes] = (out[0] + out[1]).astype(o_ref.dtype)
                if want_lse:
                    lse_ref[0, sub, rows, lanes] = jnp.where(upper, lses[1], lses[0])


def _banded(q_arr, k_arr, v_arr, q_col, k_col, v_col, *, head_of, kv_heads, past, slopes, dist_scale,
            n_sub, n_qb, sinks, want_lse, out_dtype, name):
    n, n_seq, tf, _ = q_arr.shape
    qw, kw, vw = 128 * len(head_of), kv_heads * HEAD_DIM, 2 * kv_heads * HEAD_DIM
    kwid = min((-(-past // BLOCK_Q) + 1) * BLOCK_Q, tf)
    tab = _band_table(slopes, dist_scale, past, kwid)
    kern = functools.partial(_banded_kernel, q_slabs=len(head_of), kv_slabs=kw // 128, head_of=head_of,
                             past=past, kwid=kwid, n_sub=n_sub, n_qb=n_qb, has_sinks=sinks is not None,
                             want_lse=want_lse)
    rows = n_qb * BLOCK_Q
    out_spec = pl.BlockSpec((1, n_sub, rows, qw), lambda b, j, i: (b, j, i, 0))
    out_shape = jax.ShapeDtypeStruct((n, n_seq, tf, qw), out_dtype)
    in_specs = [
        pl.BlockSpec(tab.shape, lambda b, j, i: (0, 0, 0)),
        pl.BlockSpec((1, n_sub, rows, qw), lambda b, j, i: (b, j, i, q_col)),
        pl.BlockSpec((1, n_sub, tf, kw), lambda b, j, i: (b, j, 0, k_col)),
        pl.BlockSpec((1, n_sub, tf, vw), lambda b, j, i: (b, j, 0, v_col)),
    ]
    args = [tab, q_arr, k_arr, v_arr]
    if sinks is not None:
        in_specs.insert(0, pl.BlockSpec(memory_space=pltpu.SMEM))
        args.insert(0, sinks)
    return pl.pallas_call(
        kern,
        grid=(n, n_seq // n_sub, tf // rows),
        in_specs=in_specs,
        out_specs=[out_spec, out_spec] if want_lse else out_spec,
        out_shape=[out_shape, jax.ShapeDtypeStruct(out_shape.shape, F32)] if want_lse else out_shape,
        compiler_params=_params("parallel", "parallel", "parallel"),
        name=name,
    )(*args)


def _swa_sink(za, cv, sinks):
    n, T, _ = za.shape
    za4 = za.reshape(n, 1, T, ZA_W)
    return _banded(za4, za4, cv, Z_CQ // (C_HEADS * HEAD_DIM), Z_CK // (C_KV * HEAD_DIM), 0,
                   head_of=C_SLAB_HEADS, kv_heads=C_KV, past=WIN_C - 1, slopes=_SLOPES[:C_HEADS], dist_scale=1,
                   n_sub=1, n_qb=BAND_QB, sinks=sinks, want_lse=False, out_dtype=BF16, name="swa_sink")


def _dilated_pair(bp, pair):
    w, r = B_PAIRS[pair]
    tf = bp.shape[2]
    n_qb = min(BAND_QB, tf // BLOCK_Q)
    n_sub = min(r, BAND_QB // n_qb)
    hs = B_SLOPE_OFF + pair * B_HEADS
    head_of = tuple((2 * s, 2 * s + 1) for s in range(B_HEADS // 2))
    return _banded(bp, bp, bp, 0, 1, 1, head_of=head_of, kv_heads=B_HEADS, past=w // r,
                   slopes=_SLOPES[hs:hs + B_HEADS], dist_scale=r, n_sub=n_sub, n_qb=n_qb, sinks=None,
                   want_lse=True, out_dtype=BF16, name=f"dilated_{pair}")


def _out_proj_kernel(x_ref, oat_ref, ob0_ref, ob1_ref, ob2_ref, l0_ref, l1_ref, l2_ref,
                     oc_ref, w_ref, g_ref, y_ref, slab_ref):
    tm = x_ref.shape[1]
    n_slab = B_W // 128

    def unfold(ref, slot):
        r = ref.shape[1]
        if r == 1:
            return ref[0, 0].astype(F32)
        for j in range(r):
            for c in range(n_slab):
                slab_ref[slot * n_slab + c, pl.ds(j, tm // r, stride=r), :] = (
                    ref[0, j, :, c * 128:(c + 1) * 128].astype(F32))
        return jnp.concatenate([slab_ref[slot * n_slab + c] for c in range(n_slab)], axis=1)

    obs = [unfold(ref, i) for i, ref in enumerate((ob0_ref, ob1_ref, ob2_ref))]
    l0, l1, l2 = [unfold(ref, 3 + i) for i, ref in enumerate((l0_ref, l1_ref, l2_ref))]
    mx = jnp.maximum(jnp.maximum(l0, l1), l2)
    e0, e1, e2 = jnp.exp(l0 - mx), jnp.exp(l1 - mx), jnp.exp(l2 - mx)
    ob = (e0 * obs[0] + e1 * obs[1] + e2 * obs[2]) / (e0 + e1 + e2)
    wa = A_HEADS * HEAD_DIM
    wb = wa + B_HEADS * HEAD_DIM
    oa = oat_ref[0].T.astype(BF16)
    mix = (_dot(oa, w_ref[0:wa, :]) + _dot(ob.astype(BF16), w_ref[wa:wb, :])
           + _dot(oc_ref[0, 0], w_ref[wb:, :]))
    y_ref[0] = x_ref[0] + _rmsnorm(mix, g_ref[...])


def _out_proj(x, oat, obs, lses, oc, w, g, l):
    n, T, D = x.shape
    tm = TM_PROJ
    wa, wc = A_HEADS * HEAD_DIM, C_HEADS * HEAD_DIM
    row = lambda width: pl.BlockSpec((1, tm, width), lambda b, i: (b, i, 0))
    folded = lambda a: pl.BlockSpec((1, a.shape[1], tm // a.shape[1], a.shape[3]), lambda b, i: (b, 0, i, 0))
    return pl.pallas_call(
        _out_proj_kernel,
        grid=(n, T // tm),
        in_specs=[row(D), pl.BlockSpec((1, wa, tm), lambda b, i: (b, 0, i))]
        + [folded(a) for a in obs] + [folded(a) for a in lses]
        + [folded(oc), _layer(w, l), pl.BlockSpec((1, D), lambda b, i: (0, 0))],
        out_specs=row(D),
        out_shape=jax.ShapeDtypeStruct((n, T, D), F32),
        scratch_shapes=[pltpu.VMEM((2 * len(B_PAIRS) * (B_W // 128), tm, 128), F32)],
        compiler_params=_params("parallel", "parallel"),
        name="out_proj",
    )(x, oat, *obs, *lses, oc, w, g)


def _ffn_kernel(x_ref, xp_ref, gpre_ref, wu_ref, cw_ref, cb_ref, wd_ref, gpost_ref, y_ref, h_ref, act_ref):
    ti = pl.program_id(1)
    tf = TF_FFN
    hp = _rmsnorm(xp_ref[0], gpre_ref[...])
    h_ref[0:FFN_HALO, :] = jnp.where(ti > 0, hp, 0.0).astype(BF16)
    h_ref[FFN_HALO:, :] = _rmsnorm(x_ref[0], gpre_ref[...]).astype(BF16)
    h = h_ref[...]

    def conv_branch(c0):
        u = _dot(h, wu_ref[:, c0:c0 + tf])
        cw = cw_ref[:, c0:c0 + tf]
        y = cw[2:3] * u + cw[1:2] * pltpu.roll(u, 1, 0) + cw[0:1] * pltpu.roll(u, 2, 0)
        return y[FFN_HALO:] + cb_ref[:, c0:c0 + tf]

    nff = D_FF // tf
    bounds = [nff * (i + 1) // FFN_DOWN_CHUNKS for i in range(FFN_DOWN_CHUNKS)]
    done = 0
    for k in range(nff):
        a = conv_branch(k * tf)
        b = conv_branch(D_FF + k * tf)
        act_ref[:, k * tf:(k + 1) * tf] = (_gelu_tanh(a) * b).astype(BF16)
        if k + 1 in bounds:
            lo, hi = done * tf, (k + 1) * tf
            part = _dot(act_ref[:, lo:hi], wd_ref[lo:hi, :])
            if k + 1 == nff:
                total = part if done == 0 else y_ref[0] + part
                y_ref[0] = x_ref[0] + _rmsnorm(total, gpost_ref[...])
            elif done == 0:
                y_ref[0] = part
            else:
                y_ref[0] += part
            done = k + 1


def _ffn(x, gpre, w_up, conv_w, conv_b, w_down, gpost, l):
    n, T, D = x.shape
    tm = TM_FFN
    halo_blocks = tm // FFN_HALO
    resident = lambda a: (_layer(a, l, pipeline_mode=pl.Buffered(1)) if a.ndim == 3 else
                          pl.BlockSpec(a.shape, lambda b, i: (0,) * a.ndim, pipeline_mode=pl.Buffered(1)))
    return pl.pallas_call(
        _ffn_kernel,
        grid=(n, T // tm),
        in_specs=[
            pl.BlockSpec((1, tm, D), lambda b, i: (b, i, 0)),
            pl.BlockSpec((1, FFN_HALO, D), lambda b, i: (b, jnp.maximum(i * halo_blocks - 1, 0), 0)),
            resident(gpre), resident(w_up), resident(conv_w), resident(conv_b), resident(w_down), resident(gpost),
        ],
        out_specs=pl.BlockSpec((1, tm, D), lambda b, i: (b, i, 0)),
        out_shape=jax.ShapeDtypeStruct((n, T, D), F32),
        scratch_shapes=[pltpu.VMEM((FFN_HALO + tm, D), BF16), pltpu.VMEM((tm, D_FF), BF16)],
        compiler_params=_params("parallel", "parallel"),
        name="ffn",
    )(x, x, gpre, w_up, conv_w, conv_b, w_down, gpost)


def _block_overlap_t(nc, ns):
    cs = np.arange(nc)[None, :] * CMP_STRIDE
    ss = np.arange(ns)[:, None] * SLC_LEN
    ov = np.minimum(cs + CMP_LEN, ss + SLC_LEN) - np.maximum(cs, ss)
    return (np.clip(ov, 0, None) / CMP_STRIDE).astype(np.float32)


def kernel(x, g_mix_pre, g_mix_post, g_ffn_pre, g_ffn_post, w_in, cmp_pos, cmp_k_w1, cmp_k_b1, cmp_k_w2,
           cmp_v_w1, cmp_v_b1, cmp_v_w2, sinks, w_out, w_up, conv_w, conv_b, w_down):
    n, T, _ = x.shape
    depth = w_in.shape[0]
    nch = T // CMP_STRIDE
    assert T % (BLOCK_Q * B_PAIRS[-1][1]) == 0 and T % TM_FFN == 0 and nch % BLOCK_Q == 0

    def cols(i):
        return w_in[:, :, IN_OFFS[i]:IN_OFFS[i + 1]]

    (aq, akc, avc, aks, avs, akw, avw, agate, bq, bk, bv, cq, ck, cv) = [cols(i) for i in range(len(IN_SIZES))]
    order = jnp.asarray(C_HEAD_ORDER)
    cq = cq.reshape(depth, D_MODEL, C_HEADS, HEAD_DIM)[:, :, order].reshape(depth, D_MODEL, C_HEADS * HEAD_DIM)
    c_row0 = (A_HEADS + B_HEADS) * HEAD_DIM
    w_out = jnp.concatenate(
        [w_out[:, :c_row0], w_out[:, c_row0:].reshape(depth, C_HEADS, HEAD_DIM, D_MODEL)[:, order].reshape(depth, -1, D_MODEL)],
        axis=1)
    pair_cols = [a[:, :, p * B_W:(p + 1) * B_W] for p in range(len(B_PAIRS)) for a in (bq, bk, bv)]
    wz = jnp.concatenate([aq, cq, akc, avc, aks, akw, ck, cv] + pair_cols, axis=-1).astype(BF16)
    wtv = jnp.concatenate([avs, avw], axis=-1).transpose(0, 2, 1).astype(BF16)
    wtg = jnp.pad(agate, ((0, 0), (0, 0), (0, GATE_ROWS - agate.shape[-1]))).transpose(0, 2, 1).astype(BF16)
    pos = jnp.broadcast_to(cmp_pos.reshape(depth, 1, CMP_LEN * HEAD_DIM), (depth, 8, CMP_LEN * HEAD_DIM)).astype(BF16)
    w1k, w1v = cmp_k_w1.astype(BF16), cmp_v_w1.astype(BF16)
    w2k = cmp_k_w2.astype(BF16)
    w2vt = cmp_v_w2.transpose(0, 2, 1).astype(BF16)
    w_out_b, w_up_b, w_down_b = w_out.astype(BF16), w_up.astype(BF16), w_down.astype(BF16)
    ovt = jnp.asarray(_block_overlap_t(nch, T // SLC_LEN), BF16)

    tables = _nsa_tables(T)
    row = lambda a: a.reshape(1, -1)
    for l in range(depth):
        z, cvp, b0, b1, b2, vt, gt, xkv = _in_proj(x, row(g_mix_pre[l]), wz, wtv, wtg, l)
        kc, vct = _compress(xkv, pos[l], w1k[l], row(cmp_k_b1[l]), w2k[l], w1v[l], row(cmp_v_b1[l]), w2vt[l])
        oat = _nsa(z, kc, vct, vt, gt, ovt, tables)
        oc = _swa_sink(z, cvp, sinks[l])
        pairs = [_dilated_pair(bp, p) for p, bp in enumerate((b0, b1, b2))]
        x = _out_proj(x, oat, [o for o, _ in pairs], [s for _, s in pairs], oc, w_out_b, row(g_mix_post[l]), l)
        x = _ffn(x, row(g_ffn_pre[l]), w_up_b, conv_w, row(conv_b[l]), w_down_b, row(g_ffn_post[l]), l)
    return x
```

```python
import functools

import numpy as np
import jax
import jax.numpy as jnp
from jax import lax
from jax.experimental import pallas as pl
from jax.experimental.pallas import tpu as pltpu

D_MODEL = 1024
HEAD_DIM = 64
BLOCK_Q = 128
A_KV, A_REP = 2, 3
A_HEADS = A_KV * A_REP
CMP_LEN, CMP_STRIDE, CMP_HIDDEN = 32, 16, 256
SLC_LEN, SLC_TOPN = 64, 8
WIN_A = 512
B_PAIRS = ((128, 1), (512, 4), (2048, 16))
B_HEADS = 4
C_KV, C_REP = 2, 3
C_HEADS = C_KV * C_REP
WIN_C = 128
D_FF = 2816
CONV_W = 3
EPS = 1e-6

N_ALIBI = C_HEADS + len(B_PAIRS) * B_HEADS + A_HEADS
B_SLOPE_OFF = C_HEADS
A_SLOPE_OFF = C_HEADS + len(B_PAIRS) * B_HEADS
IN_SIZES = ((A_HEADS * HEAD_DIM,) + (A_KV * HEAD_DIM,) * 6 + (A_HEADS * 3,)
            + (len(B_PAIRS) * B_HEADS * HEAD_DIM,) * 3
            + (C_HEADS * HEAD_DIM, C_KV * HEAD_DIM, C_KV * HEAD_DIM))
IN_OFFS = tuple(int(v) for v in np.cumsum((0,) + IN_SIZES))

_SLOPES = tuple(float(v) for v in
                np.exp2(-8.0 * np.arange(1, N_ALIBI + 1, dtype=np.float32) / N_ALIBI).astype(np.float32))

Z_AQ, Z_CQ = 0, 384
Z_AKC, Z_AVC, Z_AKS, Z_AKW, Z_CK = 768, 896, 1024, 1152, 1280
ZA_W = 1408
W_CV = 1408
W_B = 1536
B_W = B_HEADS * HEAD_DIM
PAIR_W = 3 * B_W
PAIR_OUT_W = 2 * B_W + 2 * B_W
W_ALL = W_B + len(B_PAIRS) * PAIR_W
GATE_ROWS = 32
C_SLAB_HEADS = tuple((r, C_REP + r) for r in range(C_REP))
C_HEAD_ORDER = tuple(h for pair in C_SLAB_HEADS for h in pair)

F32 = jnp.float32
BF16 = jnp.bfloat16
NEG_INF = float("-inf")

TM_PROJ = 512
TM_IN = 1024
TM_FFN = 1024
TF_FFN = 256
FFN_HALO = 16
FFN_DOWN_CHUNKS = 3
BAND_QB = 8
SLC_CHUNK = 512
VT_ROWS = 80
VMEM_LIMIT = 56 * 1024 * 1024


def _dot(a, b):
    return jnp.dot(a, b, preferred_element_type=F32)


def _dot_nt(a, b):
    return lax.dot_general(a, b, (((1,), (1,)), ((), ())), preferred_element_type=F32)


def _rmsnorm(x, g):
    return x * lax.rsqrt(jnp.mean(x * x, axis=-1, keepdims=True) + EPS) * g


def _gelu_tanh(x):
    return 0.5 * x * (1.0 + jnp.tanh(0.7978845608028654 * (x + 0.044715 * (x * x * x))))


def _params(*sem):
    return pltpu.CompilerParams(dimension_semantics=sem, vmem_limit_bytes=VMEM_LIMIT)


def _layer(a, l, **kw):
    return pl.BlockSpec((None,) + a.shape[1:], lambda *_: (l,) + (0,) * (a.ndim - 1), **kw)


def _pad_values(v):
    upper = lax.broadcasted_iota(jnp.int32, (v.shape[0], 128), 1) >= HEAD_DIM
    parts = []
    for s in range(v.shape[1] // 128):
        slab = v[:, s * 128:(s + 1) * 128].astype(BF16)
        parts += [jnp.where(upper, jnp.zeros_like(slab), slab), jnp.where(upper, slab, jnp.zeros_like(slab))]
    return jnp.concatenate(parts, axis=1)


def _in_proj_kernel(x_ref, g_ref, w_ref, wtv_ref, wtg_ref, za_ref, cv_ref, b0_ref, b1_ref, b2_ref,
                    vt_ref, gt_ref, xkv_ref, slab_ref):
    h = _rmsnorm(x_ref[0], g_ref[...]).astype(BF16)
    tm = h.shape[0]
    za_ref[0, :, 0:Z_AKC] = _dot(h, w_ref[:, 0:Z_AKC]).astype(BF16)
    rest = _dot(h, w_ref[:, Z_AKC:ZA_W])
    za_ref[0, :, Z_AKC:ZA_W] = rest.astype(BF16)
    n_cmp = 2 * A_KV * HEAD_DIM // 128
    for s in range(n_cmp):
        slab_ref[s] = rest[:, s * 128:(s + 1) * 128]
    taps = [[slab_ref[s, pl.ds(p, tm // CMP_STRIDE, stride=CMP_STRIDE), :] for p in range(CMP_STRIDE)]
            for s in range(n_cmp)]
    for c in range(2 * A_KV):
        lanes = slice((c % 2) * HEAD_DIM, (c % 2 + 1) * HEAD_DIM)
        xkv_ref[0, c] = jnp.concatenate([t[:, lanes] for t in taps[c // 2]], axis=1).astype(BF16)
    cv_ref[0, 0] = _pad_values(_dot(h, w_ref[:, W_CV:W_B]))
    n_slab = PAIR_W // 128
    for p, b_ref in enumerate((b0_ref, b1_ref, b2_ref)):
        r = B_PAIRS[p][1]
        res = _dot(h, w_ref[:, W_B + p * PAIR_W:W_B + (p + 1) * PAIR_W])
        if r == 1:
            b_ref[0, 0] = jnp.concatenate([res[:, 0:2 * B_W].astype(BF16), _pad_values(res[:, 2 * B_W:])], axis=1)
            continue
        for c in range(n_slab):
            slab_ref[c] = res[:, c * 128:(c + 1) * 128]
        for j in range(r):
            rows = jnp.concatenate([slab_ref[c, pl.ds(j, tm // r, stride=r), :] for c in range(n_slab)], axis=1)
            b_ref[0, j] = jnp.concatenate([rows[:, 0:2 * B_W].astype(BF16), _pad_values(rows[:, 2 * B_W:])], axis=1)
    vt = _dot_nt(wtv_ref[...], h).astype(BF16)
    fill = VT_ROWS - HEAD_DIM
    ones_row = (lax.broadcasted_iota(jnp.int32, (fill, vt.shape[1]), 0) == 0).astype(BF16)
    for i in range(2 * A_KV):
        vt_ref[0, i * VT_ROWS:i * VT_ROWS + HEAD_DIM, :] = vt[i * HEAD_DIM:(i + 1) * HEAD_DIM]
        vt_ref[0, i * VT_ROWS + HEAD_DIM:(i + 1) * VT_ROWS, :] = ones_row
    gt_ref[0] = _dot_nt(wtg_ref[...], h)


def _in_proj(x, g, wz, wtv, wtg, l):
    n, T, D = x.shape
    tm = TM_IN
    return pl.pallas_call(
        _in_proj_kernel,
        grid=(n, T // tm),
        in_specs=[
            pl.BlockSpec((1, tm, D), lambda b, i: (b, i, 0)),
            pl.BlockSpec((1, D), lambda b, i: (0, 0)),
            _layer(wz, l, pipeline_mode=pl.Buffered(1)), _layer(wtv, l), _layer(wtg, l),
        ],
        out_specs=[
            pl.BlockSpec((1, tm, ZA_W), lambda b, i: (b, i, 0)),
            pl.BlockSpec((1, 1, tm, 2 * C_KV * HEAD_DIM), lambda b, i: (b, 0, i, 0)),
        ] + [
            pl.BlockSpec((1, r, tm // r, PAIR_OUT_W), lambda b, i: (b, 0, i, 0)) for _, r in B_PAIRS
        ] + [
            pl.BlockSpec((1, 2 * A_KV * VT_ROWS, tm), lambda b, i: (b, 0, i)),
            pl.BlockSpec((1, GATE_ROWS, tm), lambda b, i: (b, 0, i)),
            pl.BlockSpec((1, 2 * A_KV, tm // CMP_STRIDE, CMP_STRIDE * HEAD_DIM), lambda b, i: (b, 0, i, 0)),
        ],
        out_shape=[
            jax.ShapeDtypeStruct((n, T, ZA_W), BF16),
            jax.ShapeDtypeStruct((n, 1, T, 2 * C_KV * HEAD_DIM), BF16),
        ] + [
            jax.ShapeDtypeStruct((n, r, T // r, PAIR_OUT_W), BF16) for _, r in B_PAIRS
        ] + [
            jax.ShapeDtypeStruct((n, 2 * A_KV * VT_ROWS, T), BF16),
            jax.ShapeDtypeStruct((n, GATE_ROWS, T), F32),
            jax.ShapeDtypeStruct((n, 2 * A_KV, T // CMP_STRIDE, CMP_STRIDE * HEAD_DIM), BF16),
        ],
        scratch_shapes=[pltpu.VMEM((PAIR_W // 128, tm, 128), F32)],
        compiler_params=_params("parallel", "parallel"),
        name="in_proj",
    )(x, g, wz, wtv, wtg)


def _compress_kernel(xk_ref, xv_ref, pos_ref, w1k_ref, b1k_ref, w2k_ref,
                     w1v_ref, b1v_ref, w2vt_ref, kc_ref, vct_ref):
    half = CMP_STRIDE * HEAD_DIM

    def hidden(x, w1_ref, b1_ref):
        y1 = _dot(x, w1_ref[0:half, :])
        y2 = _dot(x, w1_ref[half:2 * half, :])
        bias = _dot(pos_ref[...], w1_ref[...])[0:1] + b1_ref[...]
        nrow = y2.shape[0]
        return _gelu_tanh(y1 + pltpu.roll(y2, nrow - 1, 0) + bias).astype(BF16)

    kc_ref[0, 0] = _dot(hidden(xk_ref[0, 0], w1k_ref, b1k_ref), w2k_ref[...]).astype(BF16)
    vct_ref[0, 0] = _dot_nt(w2vt_ref[...], hidden(xv_ref[0, 0], w1v_ref, b1v_ref)).astype(BF16)


def _compress(xkv, pos, w1k, b1k, w2k, w1v, b1v, w2vt):
    n, _, nch, kdim = xkv.shape
    full = lambda shape: pl.BlockSpec(shape, lambda b, g: (0,) * len(shape))
    return pl.pallas_call(
        _compress_kernel,
        grid=(n, A_KV),
        in_specs=[
            pl.BlockSpec((1, 1, nch, kdim), lambda b, g: (b, g, 0, 0)),
            pl.BlockSpec((1, 1, nch, kdim), lambda b, g: (b, A_KV + g, 0, 0)),
            full(pos.shape), full(w1k.shape), full(b1k.shape), full(w2k.shape),
            full(w1v.shape), full(b1v.shape), full(w2vt.shape),
        ],
        out_specs=[
            pl.BlockSpec((1, 1, nch, HEAD_DIM), lambda b, g: (b, g, 0, 0)),
            pl.BlockSpec((1, 1, HEAD_DIM, nch), lambda b, g: (b, g, 0, 0)),
        ],
        out_shape=[
            jax.ShapeDtypeStruct((n, A_KV, nch, HEAD_DIM), BF16),
            jax.ShapeDtypeStruct((n, A_KV, HEAD_DIM, nch), BF16),
        ],
        compiler_params=_params("parallel", "parallel"),
        name="compress",
    )(xkv, xkv, pos, w1k, b1k, w2k, w1v, b1v, w2vt)


def _nsa_tables(T):
    sl = np.asarray(_SLOPES[A_SLOPE_OFF:A_SLOPE_OFF + A_HEADS], np.float32).reshape(A_KV, A_REP)
    slope = np.repeat(sl, BLOCK_Q, axis=1)[:, None, :]
    tl = np.tile(np.arange(BLOCK_Q, dtype=np.float32), A_REP)[None, None, :]
    rows = lambda n: np.arange(n, dtype=np.float32)[None, :, None]
    neg = np.float32(NEG_INF)
    p0 = slope * rows(SLC_LEN)
    pj = slope * (rows(T // SLC_LEN) * SLC_LEN)
    pc = slope * (rows(T // CMP_STRIDE) * CMP_STRIDE + np.float32((CMP_LEN - 1) / 2))
    kl = rows(BLOCK_Q)
    pd = np.where(kl <= tl, slope * (kl % SLC_LEN), neg)
    npb = -(-(WIN_A - 1) // BLOCK_Q)
    d = npb * BLOCK_Q + tl - rows((2 * npb + 1) * BLOCK_Q)
    bwin = np.where((d >= 0) & (d <= WIN_A - 1), -slope * d, neg)
    return [jnp.asarray(a.astype(np.float32)) for a in (p0, pj, pc, pd, bwin)]


def _nsa_kernel(q_ref, kc_ref, vct_ref, ks_ref, kw_ref, vt_ref, gt_ref, ovt_ref,
                p0_ref, pj_ref, pc_ref, pd_ref, bwin_ref, o_ref,
                selp_ref, s_ref, p_ref, acc_ref, m_ref, sw_ref, pw_ref):
    qb = pl.program_id(1)
    t0 = pl.multiple_of(qb * BLOCK_Q, BLOCK_Q)
    nl = A_REP * BLOCK_Q
    T = ks_ref.shape[1]
    ncmp = kc_ref.shape[2]
    nsel = ovt_ref.shape[0]
    lane = lax.broadcasted_iota(jnp.int32, (1, nl), 1)
    t_row = t0 + (lane & (BLOCK_Q - 1))
    sig = jax.nn.sigmoid(gt_ref[0])
    q = q_ref[0]
    blocks_per_q = BLOCK_Q // SLC_LEN

    groups = []
    for g in range(A_KV):
        heads = [g * A_REP + r for r in range(A_REP)]
        q3 = jnp.concatenate([q[:, h * HEAD_DIM:(h + 1) * HEAD_DIM] for h in heads], axis=0) * 0.125
        kcols = slice(g * HEAD_DIM, (g + 1) * HEAD_DIM)
        vs_rows = slice(g * VT_ROWS, (g + 1) * VT_ROWS)
        vw_rows = slice((A_KV + g) * VT_ROWS, (A_KV + g + 1) * VT_ROWS)

        st = _dot_nt(kc_ref[0, g], q3) + pc_ref[g]
        cst = lax.broadcasted_iota(jnp.int32, (ncmp, nl), 0) * CMP_STRIDE
        st = jnp.where((cst + (CMP_LEN - 1)) <= t_row, st, NEG_INF)
        m = jnp.max(st, axis=0, keepdims=True)
        m = jnp.where(m == NEG_INF, 0.0, m)
        p = jnp.exp(st - m)
        den = jnp.sum(p, axis=0, keepdims=True)
        pc = p / jnp.where(den > 0, den, 1.0)
        o_cmp = _dot(vct_ref[0, g], pc.astype(BF16))
        psum = pc[:, 0:BLOCK_Q] + pc[:, BLOCK_Q:2 * BLOCK_Q] + pc[:, 2 * BLOCK_Q:3 * BLOCK_Q]
        imp = _dot(ovt_ref[...], psum.astype(BF16))

        j = lax.broadcasted_iota(jnp.int32, (nsel, BLOCK_Q), 0)
        t128 = t0 + lax.broadcasted_iota(jnp.int32, (nsel, BLOCK_Q), 1)
        cur = t128 // SLC_LEN
        forced = (j == 0) | (j == cur) | (j == cur - 1)
        started = j * SLC_LEN <= t128
        score = jnp.where(started, jnp.where(forced, float("inf"), imp), NEG_INF)
        ahead = jnp.zeros((nsel, BLOCK_Q), jnp.int32)
        for k in range(nsel):
            rk = score[k:k + 1, :]
            ahead = ahead + ((rk > score) | ((rk == score) & (j > k))).astype(jnp.int32)
        keep = jnp.where((ahead < SLC_TOPN) & started, 0.0, NEG_INF)
        seln = jnp.concatenate([keep] * A_REP, axis=1) + pj_ref[g]
        jrow = lax.broadcasted_iota(jnp.int32, (nsel, nl), 0)
        selp_ref[g] = jnp.where(jrow < blocks_per_q * qb, seln, NEG_INF)

        sd = _dot_nt(ks_ref[0, pl.ds(t0, BLOCK_Q), kcols], q3) + pd_ref[g]
        per_block = pj_ref[g, 1:2, :]
        sd = sd + jnp.concatenate(
            [jnp.broadcast_to(per_block * (blocks_per_q * qb + b).astype(F32), (SLC_LEN, nl))
             for b in range(blocks_per_q)], axis=0)
        m_d = jnp.max(sd, axis=0, keepdims=True)
        m_ref[g] = m_d
        acc_ref[g] = jnp.zeros(acc_ref.shape[1:], F32)

        npb = -(-(WIN_A - 1) // BLOCK_Q)
        kwid = (npb + 1) * BLOCK_Q
        kst = pl.multiple_of(jnp.maximum(qb - npb, 0) * BLOCK_Q, BLOCK_Q)
        off = pl.multiple_of(kst + (npb - qb) * BLOCK_Q, BLOCK_Q)
        mx = None
        for c in range(0, kwid, BLOCK_Q):
            sc = (_dot_nt(kw_ref[0, pl.ds(kst + c, BLOCK_Q), kcols], q3)
                  + bwin_ref[g, pl.ds(off + c, BLOCK_Q), :])
            sw_ref[g, c:c + BLOCK_Q, :] = sc
            half = jnp.maximum(sc[0:BLOCK_Q // 2], sc[BLOCK_Q // 2:])
            mx = half if mx is None else jnp.maximum(mx, half)
        m = jnp.max(mx, axis=0, keepdims=True)
        for c in range(0, kwid, BLOCK_Q):
            pw_ref[g, c:c + BLOCK_Q, :] = jnp.exp(sw_ref[g, c:c + BLOCK_Q, :] - m).astype(BF16)
        acc = _dot(vt_ref[0, vw_rows, pl.ds(kst, kwid)], pw_ref[g])
        o_win = acc[0:HEAD_DIM] / acc[HEAD_DIM:HEAD_DIM + 1]
        groups.append((heads, q3, kcols, vs_rows, sd, m_d, o_cmp, o_win))

    def past(kv):
        for g, (heads, q3, kcols, vs_rows, sd, m_d, o_cmp, o_win) in enumerate(groups):
            p0 = p0_ref[g]
            mx = jnp.full((SLC_LEN, nl), NEG_INF, F32)
            for c0 in range(0, kv, SLC_CHUNK):
                s = _dot_nt(ks_ref[0, c0:c0 + SLC_CHUNK, kcols], q3)
                for b in range(SLC_CHUNK // SLC_LEN):
                    jb = c0 // SLC_LEN + b
                    sb = s[b * SLC_LEN:(b + 1) * SLC_LEN] + p0 + selp_ref[g, jb:jb + 1, :]
                    s_ref[g, jb * SLC_LEN:(jb + 1) * SLC_LEN, :] = sb
                    mx = jnp.maximum(mx, sb)
            m_all = jnp.maximum(m_d, jnp.max(mx, axis=0, keepdims=True))
            for c0 in range(0, kv, SLC_CHUNK):
                p_ref[g, c0:c0 + SLC_CHUNK, :] = jnp.exp(s_ref[g, c0:c0 + SLC_CHUNK, :] - m_all).astype(BF16)
            acc_ref[g] = _dot(vt_ref[0, vs_rows, 0:kv], p_ref[g, 0:kv, :])
            m_ref[g] = m_all

    q_per_chunk = SLC_CHUNK // BLOCK_Q
    for v in range(1, T // SLC_CHUNK + 1):
        pl.when((qb > q_per_chunk * (v - 1)) & (qb <= q_per_chunk * v))(functools.partial(past, v * SLC_CHUNK))

    for g, (heads, q3, kcols, vs_rows, sd, m_d, o_cmp, o_win) in enumerate(groups):
        p_d = jnp.exp(sd - m_ref[g]).astype(BF16)
        acc = acc_ref[g] + _dot(vt_ref[0, vs_rows, pl.ds(t0, BLOCK_Q)], p_d)
        o_slc = acc[0:HEAD_DIM] / acc[HEAD_DIM:HEAD_DIM + 1]

        for r, h in enumerate(heads):
            cols = slice(r * BLOCK_Q, (r + 1) * BLOCK_Q)
            o = (sig[3 * h:3 * h + 1] * o_cmp[:, cols] + sig[3 * h + 1:3 * h + 2] * o_slc[:, cols]
                 + sig[3 * h + 2:3 * h + 3] * o_win[:, cols])
            o_ref[0, h * HEAD_DIM:(h + 1) * HEAD_DIM, :] = o


def _nsa(z, kc, vct, vt, gt, ovt, tables):
    n, T, _ = z.shape
    ncmp = kc.shape[2]
    nsel = ovt.shape[0]
    kv_w = A_KV * HEAD_DIM
    nl = A_REP * BLOCK_Q
    win_rows = (-(-(WIN_A - 1) // BLOCK_Q) + 1) * BLOCK_Q
    const = lambda a: pl.BlockSpec(a.shape, lambda b, i: (0,) * a.ndim)
    return pl.pallas_call(
        _nsa_kernel,
        grid=(n, T // BLOCK_Q),
        in_specs=[
            pl.BlockSpec((1, BLOCK_Q, A_HEADS * HEAD_DIM), lambda b, i: (b, i, Z_AQ // (A_HEADS * HEAD_DIM))),
            pl.BlockSpec((1, A_KV, ncmp, HEAD_DIM), lambda b, i: (b, 0, 0, 0)),
            pl.BlockSpec((1, A_KV, HEAD_DIM, ncmp), lambda b, i: (b, 0, 0, 0)),
            pl.BlockSpec((1, T, kv_w), lambda b, i: (b, 0, Z_AKS // kv_w)),
            pl.BlockSpec((1, T, kv_w), lambda b, i: (b, 0, Z_AKW // kv_w)),
            pl.BlockSpec((1, 2 * A_KV * VT_ROWS, T), lambda b, i: (b, 0, 0)),
            pl.BlockSpec((1, GATE_ROWS, BLOCK_Q), lambda b, i: (b, 0, i)),
            const(ovt),
        ] + [const(a) for a in tables],
        out_specs=pl.BlockSpec((1, A_HEADS * HEAD_DIM, BLOCK_Q), lambda b, i: (b, 0, i)),
        out_shape=jax.ShapeDtypeStruct((n, A_HEADS * HEAD_DIM, T), F32),
        scratch_shapes=[
            pltpu.VMEM((A_KV, nsel, nl), F32),
            pltpu.VMEM((A_KV, T, nl), F32), pltpu.VMEM((A_KV, T, nl), BF16),
            pltpu.VMEM((A_KV, VT_ROWS, nl), F32), pltpu.VMEM((A_KV, 1, nl), F32),
            pltpu.VMEM((A_KV, win_rows, nl), F32), pltpu.VMEM((A_KV, win_rows, nl), BF16),
        ],
        compiler_params=_params("parallel", "parallel"),
        name="nsa",
    )(z, kc, vct, z, z, vt, gt, ovt, *tables)


def _band_table(slopes, dist_scale, past, kwid):
    npb = -(-past // BLOCK_Q)
    i = np.arange(BLOCK_Q)[:, None]
    j = np.arange(kwid + npb * BLOCK_Q)[None, :]
    diff = npb * BLOCK_Q + i - j
    sl = np.asarray(slopes, np.float32)[:, None, None]
    tab = np.where((diff >= 0) & (diff <= past), -sl * (diff * dist_scale).astype(np.float32), np.float32(NEG_INF))
    return jnp.asarray(tab.astype(np.float32))


def _banded_kernel(*refs, q_slabs, kv_slabs, head_of, past, kwid, n_sub, n_qb, has_sinks, want_lse):
    refs = list(refs)
    sink_ref = refs.pop(0) if has_sinks else None
    tab_ref, q_ref, k_ref, v_ref, o_ref = refs[:5]
    lse_ref = refs[5] if want_lse else None
    npb = -(-past // BLOCK_Q)
    cr = min(BLOCK_Q, 16 * 1024 // kwid)
    upper = lax.broadcasted_iota(jnp.int32, (cr, 128), 1) >= HEAD_DIM
    ones = jnp.ones((kwid, 128), BF16)
    for sub in range(n_sub):
        for part in range(n_qb * BLOCK_Q // cr):
            qb = pl.program_id(2) * n_qb + part * cr // BLOCK_Q
            kst = pl.multiple_of(jnp.maximum(qb - npb, 0) * BLOCK_Q, BLOCK_Q)
            off = pl.multiple_of(kst + (npb - qb) * BLOCK_Q, BLOCK_Q)
            rows = slice(part * cr, (part + 1) * cr)
            trow = slice(part * cr % BLOCK_Q, part * cr % BLOCK_Q + cr)
            for s_out in range(q_slabs):
                lanes = slice(s_out * 128, (s_out + 1) * 128)
                q = q_ref[0, sub, rows, lanes]
                kv = s_out * kv_slabs // q_slabs
                kwin = k_ref[0, sub, pl.ds(kst, kwid), kv * 128:(kv + 1) * 128]
                out, lses = [], []
                for half in range(2):
                    h = head_of[s_out][half]
                    qm = jnp.where(upper == (half == 1), q, jnp.zeros_like(q)) * 0.125
                    sr = _dot_nt(qm, kwin) + tab_ref[h, trow, pl.ds(off, kwid)]
                    m = jnp.max(sr, axis=1, keepdims=True)
                    if has_sinks:
                        m = jnp.maximum(m, sink_ref[h])
                    p = jnp.exp(sr - m).astype(BF16)
                    vpad = v_ref[0, sub, pl.ds(kst, kwid), (2 * kv + half) * 128:(2 * kv + half + 1) * 128]
                    den = _dot(p, ones)
                    if has_sinks:
                        den = den + jnp.exp(sink_ref[h] - m)
                    out.append(_dot(p, vpad) / den)
                    lses.append(m + jnp.log(den))
                o_ref[0, sub, rows, lanes] = (out[0] + out[1]).astype(o_ref.dtype)
                if want_lse:
                    lse_ref[0, sub, rows, lanes] = jnp.where(upper, lses[1], lses[0])


def _banded(q_arr, k_arr, v_arr, q_col, k_col, v_col, *, head_of, kv_heads, past, slopes, dist_scale,
            n_sub, n_qb, sinks, want_lse, out_dtype, name):
    n, n_seq, tf, _ = q_arr.shape
    qw, kw, vw = 128 * len(head_of), kv_heads * HEAD_DIM, 2 * kv_heads * HEAD_DIM
    kwid = min((-(-past // BLOCK_Q) + 1) * BLOCK_Q, tf)
    tab = _band_table(slopes, dist_scale, past, kwid)
    kern = functools.partial(_banded_kernel, q_slabs=len(head_of), kv_slabs=kw // 128, head_of=head_of,
                             past=past, kwid=kwid, n_sub=n_sub, n_qb=n_qb, has_sinks=sinks is not None,
                             want_lse=want_lse)
    rows = n_qb * BLOCK_Q
    out_spec = pl.BlockSpec((1, n_sub, rows, qw), lambda b, j, i: (b, j, i, 0))
    out_shape = jax.ShapeDtypeStruct((n, n_seq, tf, qw), out_dtype)
    in_specs = [
        pl.BlockSpec(tab.shape, lambda b, j, i: (0, 0, 0)),
        pl.BlockSpec((1, n_sub, rows, qw), lambda b, j, i: (b, j, i, q_col)),
        pl.BlockSpec((1, n_sub, tf, kw), lambda b, j, i: (b, j, 0, k_col)),
        pl.BlockSpec((1, n_sub, tf, vw), lambda b, j, i: (b, j, 0, v_col)),
    ]
    args = [tab, q_arr, k_arr, v_arr]
    if sinks is not None:
        in_specs.insert(0, pl.BlockSpec(memory_space=pltpu.SMEM))
        args.insert(0, sinks)
    return pl.pallas_call(
        kern,
        grid=(n, n_seq // n_sub, tf // rows),
        in_specs=in_specs,
        out_specs=[out_spec, out_spec] if want_lse else out_spec,
        out_shape=[out_shape, jax.ShapeDtypeStruct(out_shape.shape, F32)] if want_lse else out_shape,
        compiler_params=_params("parallel", "parallel", "parallel"),
        name=name,
    )(*args)


def _swa_sink(za, cv, sinks):
    n, T, _ = za.shape
    za4 = za.reshape(n, 1, T, ZA_W)
    return _banded(za4, za4, cv, Z_CQ // (C_HEADS * HEAD_DIM), Z_CK // (C_KV * HEAD_DIM), 0,
                   head_of=C_SLAB_HEADS, kv_heads=C_KV, past=WIN_C - 1, slopes=_SLOPES[:C_HEADS], dist_scale=1,
                   n_sub=1, n_qb=BAND_QB, sinks=sinks, want_lse=False, out_dtype=BF16, name="swa_sink")


def _dilated_pair(bp, pair):
    w, r = B_PAIRS[pair]
    tf = bp.shape[2]
    n_qb = min(BAND_QB, tf // BLOCK_Q)
    n_sub = min(r, BAND_QB // n_qb)
    hs = B_SLOPE_OFF + pair * B_HEADS
    head_of = tuple((2 * s, 2 * s + 1) for s in range(B_HEADS // 2))
    return _banded(bp, bp, bp, 0, 1, 1, head_of=head_of, kv_heads=B_HEADS, past=w // r,
                   slopes=_SLOPES[hs:hs + B_HEADS], dist_scale=r, n_sub=n_sub, n_qb=n_qb, sinks=None,
                   want_lse=True, out_dtype=BF16, name=f"dilated_{pair}")


def _out_proj_kernel(x_ref, oat_ref, ob0_ref, ob1_ref, ob2_ref, l0_ref, l1_ref, l2_ref,
                     oc_ref, w_ref, g_ref, y_ref, slab_ref):
    tm = x_ref.shape[1]
    n_slab = B_W // 128

    def unfold(ref, slot):
        r = ref.shape[1]
        if r == 1:
            return ref[0, 0].astype(F32)
        for j in range(r):
            for c in range(n_slab):
                slab_ref[slot * n_slab + c, pl.ds(j, tm // r, stride=r), :] = (
                    ref[0, j, :, c * 128:(c + 1) * 128].astype(F32))
        return jnp.concatenate([slab_ref[slot * n_slab + c] for c in range(n_slab)], axis=1)

    obs = [unfold(ref, i) for i, ref in enumerate((ob0_ref, ob1_ref, ob2_ref))]
    l0, l1, l2 = [unfold(ref, 3 + i) for i, ref in enumerate((l0_ref, l1_ref, l2_ref))]
    mx = jnp.maximum(jnp.maximum(l0, l1), l2)
    e0, e1, e2 = jnp.exp(l0 - mx), jnp.exp(l1 - mx), jnp.exp(l2 - mx)
    ob = (e0 * obs[0] + e1 * obs[1] + e2 * obs[2]) / (e0 + e1 + e2)
    wa = A_HEADS * HEAD_DIM
    wb = wa + B_HEADS * HEAD_DIM
    oa = oat_ref[0].T.astype(BF16)
    mix = (_dot(oa, w_ref[0:wa, :]) + _dot(ob.astype(BF16), w_ref[wa:wb, :])
           + _dot(oc_ref[0, 0], w_ref[wb:, :]))
    y_ref[0] = x_ref[0] + _rmsnorm(mix, g_ref[...])


def _out_proj(x, oat, obs, lses, oc, w, g, l):
    n, T, D = x.shape
    tm = TM_PROJ
    wa, wc = A_HEADS * HEAD_DIM, C_HEADS * HEAD_DIM
    row = lambda width: pl.BlockSpec((1, tm, width), lambda b, i: (b, i, 0))
    folded = lambda a: pl.BlockSpec((1, a.shape[1], tm // a.shape[1], a.shape[3]), lambda b, i: (b, 0, i, 0))
    return pl.pallas_call(
        _out_proj_kernel,
        grid=(n, T // tm),
        in_specs=[row(D), pl.BlockSpec((1, wa, tm), lambda b, i: (b, 0, i))]
        + [folded(a) for a in obs] + [folded(a) for a in lses]
        + [folded(oc), _layer(w, l), pl.BlockSpec((1, D), lambda b, i: (0, 0))],
        out_specs=row(D),
        out_shape=jax.ShapeDtypeStruct((n, T, D), F32),
        scratch_shapes=[pltpu.VMEM((2 * len(B_PAIRS) * (B_W // 128), tm, 128), F32)],
        compiler_params=_params("parallel", "parallel"),
        name="out_proj",
    )(x, oat, *obs, *lses, oc, w, g)


def _ffn_kernel(x_ref, xp_ref, gpre_ref, wu_ref, cw_ref, cb_ref, wd_ref, gpost_ref, y_ref, h_ref, act_ref):
    ti = pl.program_id(1)
    tf = TF_FFN
    hp = _rmsnorm(xp_ref[0], gpre_ref[...])
    h_ref[0:FFN_HALO, :] = jnp.where(ti > 0, hp, 0.0).astype(BF16)
    h_ref[FFN_HALO:, :] = _rmsnorm(x_ref[0], gpre_ref[...]).astype(BF16)
    h = h_ref[...]

    def conv_branch(c0):
        u = _dot(h, wu_ref[:, c0:c0 + tf])
        cw = cw_ref[:, c0:c0 + tf]
        y = cw[2:3] * u + cw[1:2] * pltpu.roll(u, 1, 0) + cw[0:1] * pltpu.roll(u, 2, 0)
        return y[FFN_HALO:] + cb_ref[:, c0:c0 + tf]

    nff = D_FF // tf
    bounds = [nff * (i + 1) // FFN_DOWN_CHUNKS for i in range(FFN_DOWN_CHUNKS)]
    done = 0
    for k in range(nff):
        a = conv_branch(k * tf)
        b = conv_branch(D_FF + k * tf)
        act_ref[:, k * tf:(k + 1) * tf] = (_gelu_tanh(a) * b).astype(BF16)
        if k + 1 in bounds:
            lo, hi = done * tf, (k + 1) * tf
            part = _dot(act_ref[:, lo:hi], wd_ref[lo:hi, :])
            if k + 1 == nff:
                total = part if done == 0 else y_ref[0] + part
                y_ref[0] = x_ref[0] + _rmsnorm(total, gpost_ref[...])
            elif done == 0:
                y_ref[0] = part
            else:
                y_ref[0] += part
            done = k + 1


def _ffn(x, gpre, w_up, conv_w, conv_b, w_down, gpost, l):
    n, T, D = x.shape
    tm = TM_FFN
    halo_blocks = tm // FFN_HALO
    resident = lambda a: (_layer(a, l, pipeline_mode=pl.Buffered(1)) if a.ndim == 3 else
                          pl.BlockSpec(a.shape, lambda b, i: (0,) * a.ndim, pipeline_mode=pl.Buffered(1)))
    return pl.pallas_call(
        _ffn_kernel,
        grid=(n, T // tm),
        in_specs=[
            pl.BlockSpec((1, tm, D), lambda b, i: (b, i, 0)),
            pl.BlockSpec((1, FFN_HALO, D), lambda b, i: (b, jnp.maximum(i * halo_blocks - 1, 0), 0)),
            resident(gpre), resident(w_up), resident(conv_w), resident(conv_b), resident(w_down), resident(gpost),
        ],
        out_specs=pl.BlockSpec((1, tm, D), lambda b, i: (b, i, 0)),
        out_shape=jax.ShapeDtypeStruct((n, T, D), F32),
        scratch_shapes=[pltpu.VMEM((FFN_HALO + tm, D), BF16), pltpu.VMEM((tm, D_FF), BF16)],
        compiler_params=_params("parallel", "parallel"),
        name="ffn",
    )(x, x, gpre, w_up, conv_w, conv_b, w_down, gpost)


def _block_overlap_t(nc, ns):
    cs = np.arange(nc)[None, :] * CMP_STRIDE
    ss = np.arange(ns)[:, None] * SLC_LEN
    ov = np.minimum(cs + CMP_LEN, ss + SLC_LEN) - np.maximum(cs, ss)
    return (np.clip(ov, 0, None) / CMP_STRIDE).astype(np.float32)


def kernel(x, g_mix_pre, g_mix_post, g_ffn_pre, g_ffn_post, w_in, cmp_pos, cmp_k_w1, cmp_k_b1, cmp_k_w2,
           cmp_v_w1, cmp_v_b1, cmp_v_w2, sinks, w_out, w_up, conv_w, conv_b, w_down):
    n, T, _ = x.shape
    depth = w_in.shape[0]
    nch = T // CMP_STRIDE
    assert T % (BLOCK_Q * B_PAIRS[-1][1]) == 0 and T % TM_FFN == 0 and nch % BLOCK_Q == 0

    def cols(i):
        return w_in[:, :, IN_OFFS[i]:IN_OFFS[i + 1]]

    (aq, akc, avc, aks, avs, akw, avw, agate, bq, bk, bv, cq, ck, cv) = [cols(i) for i in range(len(IN_SIZES))]
    order = jnp.asarray(C_HEAD_ORDER)
    cq = cq.reshape(depth, D_MODEL, C_HEADS, HEAD_DIM)[:, :, order].reshape(depth, D_MODEL, C_HEADS * HEAD_DIM)
    c_row0 = (A_HEADS + B_HEADS) * HEAD_DIM
    w_out = jnp.concatenate(
        [w_out[:, :c_row0], w_out[:, c_row0:].reshape(depth, C_HEADS, HEAD_DIM, D_MODEL)[:, order].reshape(depth, -1, D_MODEL)],
        axis=1)
    pair_cols = [a[:, :, p * B_W:(p + 1) * B_W] for p in range(len(B_PAIRS)) for a in (bq, bk, bv)]
    wz = jnp.concatenate([aq, cq, akc, avc, aks, akw, ck, cv] + pair_cols, axis=-1).astype(BF16)
    wtv = jnp.concatenate([avs, avw], axis=-1).transpose(0, 2, 1).astype(BF16)
    wtg = jnp.pad(agate, ((0, 0), (0, 0), (0, GATE_ROWS - agate.shape[-1]))).transpose(0, 2, 1).astype(BF16)
    pos = jnp.broadcast_to(cmp_pos.reshape(depth, 1, CMP_LEN * HEAD_DIM), (depth, 8, CMP_LEN * HEAD_DIM)).astype(BF16)
    w1k, w1v = cmp_k_w1.astype(BF16), cmp_v_w1.astype(BF16)
    w2k = cmp_k_w2.astype(BF16)
    w2vt = cmp_v_w2.transpose(0, 2, 1).astype(BF16)
    w_out_b, w_up_b, w_down_b = w_out.astype(BF16), w_up.astype(BF16), w_down.astype(BF16)
    ovt = jnp.asarray(_block_overlap_t(nch, T // SLC_LEN), BF16)

    tables = _nsa_tables(T)
    row = lambda a: a.reshape(1, -1)
    for l in range(depth):
        z, cvp, b0, b1, b2, vt, gt, xkv = _in_proj(x, row(g_mix_pre[l]), wz, wtv, wtg, l)
        kc, vct = _compress(xkv, pos[l], w1k[l], row(cmp_k_b1[l]), w2k[l], w1v[l], row(cmp_v_b1[l]), w2vt[l])
        oat = _nsa(z, kc, vct, vt, gt, ovt, tables)
        oc = _swa_sink(z, cvp, sinks[l])
        pairs = [_dilated_pair(bp, p) for p, bp in enumerate((b0, b1, b2))]
        x = _out_proj(x, oat, [o for o, _ in pairs], [s for _, s in pairs], oc, w_out_b, row(g_mix_post[l]), l)
        x = _ffn(x, row(g_ffn_pre[l]), w_up_b, conv_w, row(conv_b[l]), w_down_b, row(g_ffn_post[l]), l)
    return x
```
